```python
import jax, jax.numpy as jnp
from jax import lax
import numpy as np

D_MODEL = 2048
BATCH = 1
SEQ = 8192
DEPTH = 4

HEAD_DIM = 128
NORM_EPS = 1e-6
SB_HEADS = 8
SB_WIDTH = SB_HEADS * HEAD_DIM
SB_BLOCK = 128
CONV_WIDTH = 1024
CONV_KERNEL = 31
SGU_WIDTH = 1024
SGU_GROUPS = 8
SGU_CHUNK = 128
DIL_PATTERNS = ((128, 1), (512, 4), (2048, 16))
DIL_NGROUPS = 3
DIL_SLOTS = 8
DIL_WIDTH = DIL_SLOTS * HEAD_DIM
DIL_BLOCK = 128
N_BRANCH = 4
BRANCH_WIDTH = 1024
IN_SIZES = (SB_WIDTH, SB_WIDTH, SB_WIDTH, SB_WIDTH,
            CONV_WIDTH, CONV_WIDTH, CONV_WIDTH,
            SGU_WIDTH, SGU_WIDTH, SGU_WIDTH,
            DIL_NGROUPS * DIL_WIDTH, DIL_NGROUPS * DIL_WIDTH,
            DIL_WIDTH, DIL_WIDTH)
IN_WIDTH = 18432

kernel_name = "hybrid_sb_conv_sgu_dilated_block"


def rmsnorm(x, g):
    xf = x.astype(jnp.float32)
    y = xf * lax.rsqrt(jnp.mean(xf * xf, axis=-1, keepdims=True) + NORM_EPS)
    return (y * g.astype(jnp.float32)).astype(x.dtype)


def layernorm(x, g, b):
    xf = x.astype(jnp.float32)
    mu = jnp.mean(xf, axis=-1, keepdims=True)
    var = jnp.mean(jnp.square(xf - mu), axis=-1, keepdims=True)
    y = (xf - mu) * lax.rsqrt(var + NORM_EPS)
    return (y * g.astype(jnp.float32) + b.astype(jnp.float32)).astype(x.dtype)


def stick_breaking_attention(q, k, v):
    B, S, H, Dh = q.shape
    nb = S // SB_BLOCK
    scale = Dh ** -0.5
    qb = q.astype(jnp.float32).reshape(B, nb, SB_BLOCK, H, Dh).transpose(1, 0, 3, 2, 4)
    kf = k.astype(jnp.float32)
    vf = v.astype(jnp.float32)
    key_pos = jnp.arange(S)

    def block(args):
        qi, n = args
        z = jnp.einsum('bhqd,bkhd->bhqk', qi, kf) * scale
        q_pos = n * SB_BLOCK + jnp.arange(SB_BLOCK)
        causal = key_pos[None, :] < q_pos[:, None]
        log_beta = jax.nn.log_sigmoid(z)
        log_1mb = jnp.where(causal, jax.nn.log_sigmoid(-z), 0.0)
        after = lax.cumsum(log_1mb, axis=3, reverse=True) - log_1mb
        w = jnp.where(causal, jnp.exp(log_beta + after), 0.0)
        return jnp.einsum('bhqk,bkhd->bqhd', w, vf)

    out = lax.map(block, (qb, jnp.arange(nb)))
    return out.transpose(1, 0, 2, 3, 4).reshape(B, S, H * Dh)


def causal_depthwise_conv(x, w, b):
    K, C = w.shape
    xp = jnp.pad(x, ((0, 0), (K - 1, 0), (0, 0)))
    y = lax.conv_general_dilated(xp, w[:, None, :], window_strides=(1,), padding='VALID',
                                 dimension_numbers=('NWC', 'WIO', 'NWC'), feature_group_count=C)
    return y + b


def conformer_conv(glu_a, glu_b, conv_w, conv_b, ln_g, ln_b):
    g = glu_a * jax.nn.sigmoid(glu_b)
    c = causal_depthwise_conv(g, conv_w, conv_b)
    return jax.nn.silu(layernorm(c, ln_g, ln_b))


def spatial_gating(u, v, w_s, b_s, ln_g, ln_b):
    B, S, C = u.shape
    n = S // SGU_CHUNK
    G = SGU_GROUPS
    vn = layernorm(v, ln_g, ln_b).reshape(B, n, SGU_CHUNK, G, C // G)
    mask = jnp.tril(jnp.ones((SGU_CHUNK, SGU_CHUNK), dtype=bool))
    w = jnp.where(mask[None], w_s, jnp.zeros_like(w_s))
    z = jnp.einsum('gts,bnsgc->bntgc', w, vn) + b_s.T[None, None, :, :, None]
    return u * z.reshape(B, S, C)


def dilated_group_attention(q, k, v, window, dil):
    B, S, H, Dh = q.shape
    n_keys = window // dil
    L = S // dil
    N = B * dil
    T = DIL_BLOCK
    nb = -(-L // T)
    Lp = nb * T

    def to_classes(t):
        t = t.astype(jnp.float32).reshape(B, L, dil, H, Dh).transpose(0, 2, 1, 3, 4).reshape(N, L, H, Dh)
        t = jnp.pad(t, ((0, 0), (0, Lp - L), (0, 0), (0, 0)))
        return t.reshape(N, nb, T, H, Dh)

    def with_prev(t):
        prev = jnp.pad(t[:, :-1], ((0, 0), (1, 0), (0, 0), (0, 0), (0, 0)))
        return jnp.concatenate([prev, t], axis=2)

    qb = to_classes(q)
    kw = with_prev(to_classes(k))
    vw = with_prev(to_classes(v))
    s = jnp.einsum('nbqhd,nbkhd->nbhqk', qb, kw) * (Dh ** -0.5)
    a = jnp.arange(T)[:, None]
    c = jnp.arange(2 * T)[None, :]
    rel = T + a - c
    blk = jnp.arange(nb)[:, None, None]
    valid = (rel >= 0) & (rel <= n_keys) & ((blk > 0) | (c >= T))
    s = jnp.where(valid[None, :, None], s, -jnp.inf)
    m = jnp.max(s, axis=-1, keepdims=True)
    p = jnp.exp(s - m)
    den = jnp.sum(p, axis=-1, keepdims=True)
    o = jnp.einsum('nbhqk,nbkhd->nbqhd', p, vw) / den.transpose(0, 1, 3, 2, 4)
    lse = (m + jnp.log(den))[..., 0].transpose(0, 1, 3, 2)

    def from_classes(t):
        tail = t.shape[3:]
        t = t.reshape((N, Lp) + tail)[:, :L]
        t = jnp.moveaxis(t.reshape((B, dil, L) + tail), 1, 2)
        return t.reshape((B, S) + tail)

    return from_classes(o), from_classes(lse)


def dilated_mixture(q, k, v):
    outs, lses = [], []
    for g, (window, dil) in enumerate(DIL_PATTERNS):
        o, lse = dilated_group_attention(q[:, :, g], k[:, :, g], v, window, dil)
        outs.append(o)
        lses.append(lse)
    w = jax.nn.softmax(jnp.stack(lses, 0), axis=0)
    return jnp.einsum('gbsh,gbshd->bshd', w, jnp.stack(outs, 0))


def hybrid_layer(x, norm_g, w_in, conv_w, conv_b, conv_ln_g, conv_ln_b, sgu_ln_g, sgu_ln_b,
                 sgu_w, sgu_b, w_branch, w_gate, b_gate, w_out):
    B, S, D = x.shape
    h = rmsnorm(x, norm_g)
    proj = jnp.einsum('bsd,df->bsf', h, w_in)
    splits = [int(i) for i in np.cumsum(IN_SIZES)[:-1]]
    (a_q, a_k, a_v, a_g, b_a, b_b, b_g, c_u, c_v, c_g, d_q, d_k, d_v, d_g) = jnp.split(proj, splits, axis=-1)

    shp = (B, S, SB_HEADS, HEAD_DIM)
    ya = stick_breaking_attention(a_q.reshape(shp), a_k.reshape(shp), a_v.reshape(shp)).astype(x.dtype)
    ya = ya * jax.nn.silu(a_g)
    yb = conformer_conv(b_a, b_b, conv_w, conv_b, conv_ln_g, conv_ln_b) * jax.nn.silu(b_g)
    yc = spatial_gating(jax.nn.gelu(c_u), jax.nn.gelu(c_v), sgu_w, sgu_b, sgu_ln_g, sgu_ln_b) * jax.nn.silu(c_g)
    gshp = (B, S, DIL_NGROUPS, DIL_SLOTS, HEAD_DIM)
    yd = dilated_mixture(d_q.reshape(gshp), d_k.reshape(gshp), d_v.reshape(B, S, DIL_SLOTS, HEAD_DIM))
    yd = yd.reshape(B, S, DIL_WIDTH).astype(x.dtype) * jax.nn.silu(d_g)

    branches = jnp.stack([ya, yb, yc, yd], axis=2)
    yproj = jnp.einsum('bsnc,ncd->bsnd', branches, w_branch)
    gates = jax.nn.sigmoid(jnp.einsum('bsd,df->bsf', h, w_gate) + b_gate).reshape(B, S, N_BRANCH, D)
    merged = jnp.sum(gates * yproj, axis=2)
    return x + jnp.einsum('bsd,de->bse', merged, w_out)


def setup_inputs(seed: int = 0) -> dict:
    key = jax.random.key(seed)
    ks = jax.random.split(key, 17)
    f32 = jnp.float32
    nrm = lambda k, shape, scale: jax.random.normal(k, shape, f32) * scale
    return {
        "x": nrm(ks[0], (BATCH, SEQ, D_MODEL), 1.0),
        "norm_g": 1.0 + nrm(ks[1], (DEPTH, D_MODEL), 0.01),
        "w_in": nrm(ks[2], (DEPTH, D_MODEL, IN_WIDTH), D_MODEL ** -0.5),
        "conv_w": nrm(ks[3], (DEPTH, CONV_KERNEL, CONV_WIDTH), CONV_KERNEL ** -0.5),
        "conv_b": nrm(ks[4], (DEPTH, CONV_WIDTH), 0.01),
        "conv_ln_g": 1.0 + nrm(ks[5], (DEPTH, CONV_WIDTH), 0.01),
        "conv_ln_b": nrm(ks[6], (DEPTH, CONV_WIDTH), 0.01),
        "sgu_ln_g": 1.0 + nrm(ks[7], (DEPTH, SGU_WIDTH), 0.01),
        "sgu_ln_b": nrm(ks[8], (DEPTH, SGU_WIDTH), 0.01),
        "sgu_w": nrm(ks[9], (DEPTH, SGU_GROUPS, SGU_CHUNK, SGU_CHUNK), SGU_CHUNK ** -0.5),
        "sgu_b": 1.0 + nrm(ks[10], (DEPTH, SGU_GROUPS, SGU_CHUNK), 0.01),
        "w_branch": nrm(ks[11], (DEPTH, N_BRANCH, BRANCH_WIDTH, D_MODEL), BRANCH_WIDTH ** -0.5),
        "w_gate": nrm(ks[12], (DEPTH, D_MODEL, N_BRANCH * D_MODEL), D_MODEL ** -0.5),
        "b_gate": nrm(ks[13], (DEPTH, N_BRANCH * D_MODEL), 0.01),
        "w_out": nrm(ks[14], (DEPTH, D_MODEL, D_MODEL), D_MODEL ** -0.5),
        "final_g": 1.0 + nrm(ks[15], (D_MODEL,), 0.01),
    }


def reference(x, norm_g, w_in, conv_w, conv_b, conv_ln_g, conv_ln_b, sgu_ln_g, sgu_ln_b,
              sgu_w, sgu_b, w_branch, w_gate, b_gate, w_out, final_g):
    for l in range(DEPTH):
        x = hybrid_layer(x, norm_g[l], w_in[l], conv_w[l], conv_b[l], conv_ln_g[l], conv_ln_b[l],
                         sgu_ln_g[l], sgu_ln_b[l], sgu_w[l], sgu_b[l], w_branch[l], w_gate[l],
                         b_gate[l], w_out[l])
    return rmsnorm(x, final_g)
```

```python
import functools

import numpy as np
import jax
import jax.numpy as jnp
from jax import lax
from jax.experimental import pallas as pl
from jax.experimental.pallas import tpu as pltpu

F32 = jnp.float32
BF16 = jnp.bfloat16

D_MODEL = 2048
DEPTH = 4
HEAD_DIM = 128
NORM_EPS = 1e-6
N_HEADS = 8
WIDTH = N_HEADS * HEAD_DIM
CONV_KERNEL = 31
CONV_HALO = 32
SGU_GROUPS = 8
SGU_CHUNK = 128
DIL_PATTERNS = ((128, 1), (512, 4), (2048, 16))
DIL_BLOCK = 128
N_BRANCH = 4
IN_WIDTH = 18432

_OFF = dict(a_q=0, a_k=1024, a_v=2048, a_g=3072, b_a=4096, b_b=5120, b_g=6144,
            c_u=7168, c_v=8192, c_g=9216, d_q=10240, d_k=13312, d_v=16384, d_g=17408)
_MM_RANGES = ((0, 3072), (10240, 17408))
_EW_RANGES = ((3072, 10240), (17408, 18432))
MM_WIDTH = 10240
EW_WIDTH = 8192
MM_Q, MM_K, MM_V, MM_DQ, MM_DK, MM_DV = 0, 1, 2, 3, 6, 9
EW_AG, EW_BA, EW_BB, EW_BG, EW_CU, EW_CV, EW_CG, EW_DG = range(8)

SB_DEAD = -104.0
VMEM_LIMIT = 56 * 1024 * 1024


def _cparams(*sem):
    return pltpu.CompilerParams(dimension_semantics=sem, vmem_limit_bytes=VMEM_LIMIT)


def _sigmoid(x):
    return 1.0 / (1.0 + jnp.exp(-x))


def _silu(x):
    return x * _sigmoid(x)


def _gelu_tanh(x):
    c = np.sqrt(2.0 / np.pi).astype(np.float32)
    return x * (0.5 * (1.0 + jnp.tanh(c * (x + 0.044715 * (x * x * x)))))


def _rmsnorm_rows(x, g):
    ms = jnp.mean(x * x, axis=-1, keepdims=True)
    return x * lax.rsqrt(ms + NORM_EPS) * g


def _layernorm_rows(x, g, b):
    mu = jnp.mean(x, axis=-1, keepdims=True)
    xc = x - mu
    var = jnp.mean(xc * xc, axis=-1, keepdims=True)
    return xc * lax.rsqrt(var + NORM_EPS) * g + b


NORM_ROWS = 256


def _norm_into(x_ref, g_ref, h_ref, rows):
    g = g_ref[...]
    for c in range(rows // NORM_ROWS):
        sl = slice(c * NORM_ROWS, (c + 1) * NORM_ROWS)
        h_ref[sl, :] = _rmsnorm_rows(x_ref[sl, :], g).astype(h_ref.dtype)


def _proj_kernel(tbl_ref, x_ref, g_ref, w_ref, o_ref, h_ref, *, tm):
    del tbl_ref

    @pl.when(pl.program_id(1) == 0)
    def _():
        _norm_into(x_ref, g_ref, h_ref, tm)

    o_ref[...] = jnp.dot(h_ref[...], w_ref[...], preferred_element_type=F32).astype(o_ref.dtype)


def _col_table(ranges, tn):
    tiles = []
    for lo, hi in ranges:
        tiles.extend(range(lo // tn, hi // tn))
    return jnp.asarray(tiles, dtype=jnp.int32)


def _project(x, norm_g, w_in, layer, ranges, out_dtype, *, tm=1024, tn=512):
    s, d = x.shape
    tbl = _col_table(ranges, tn)
    n_tiles = tbl.shape[0]
    return pl.pallas_call(
        functools.partial(_proj_kernel, tm=tm),
        grid_spec=pltpu.PrefetchScalarGridSpec(
            num_scalar_prefetch=1,
            grid=(s // tm, n_tiles),
            in_specs=[
                pl.BlockSpec((tm, d), lambda i, j, t: (i, 0)),
                pl.BlockSpec((None, 1, d), lambda i, j, t: (layer, 0, 0)),
                pl.BlockSpec((None, d, tn), lambda i, j, t: (layer, 0, t[j])),
            ],
            out_specs=pl.BlockSpec((tm, tn), lambda i, j, t: (i, j)),
            scratch_shapes=[pltpu.VMEM((tm, d), BF16)],
        ),
        out_shape=jax.ShapeDtypeStruct((s, n_tiles * tn), out_dtype),
        compiler_params=_cparams("parallel", "arbitrary"),
        name="in_proj",
    )(tbl, x, norm_g, w_in)


SB_BLOCK = 128


def _sb_kernel(uu_ref, q_ref, k_ref, v_ref, g_ref, o_ref, carry_ref, acc_ref):
    blk = SB_BLOCK
    i = pl.program_id(1)
    q = q_ref[...]
    uu = uu_ref[...]
    scale = HEAD_DIM ** -0.5

    def step(j, mask):
        start = pl.multiple_of(j * blk, blk)
        k = k_ref[pl.ds(start, blk), :]
        v = v_ref[pl.ds(start, blk), :]
        z = lax.dot_general(q, k, (((1,), (1,)), ((), ())), preferred_element_type=F32) * scale
        sp = jnp.maximum(z, 0.0) + jnp.log1p(jnp.exp(-jnp.abs(z)))
        l1mb = -sp
        if mask is not None:
            l1mb = jnp.where(mask, l1mb, 0.0)
        hi = l1mb.astype(BF16)
        lo = (l1mb - hi.astype(F32)).astype(BF16)
        r = jnp.dot(jnp.concatenate([hi, lo], axis=1), uu, preferred_element_type=F32)
        carry = carry_ref[...]
        p = jnp.exp((z - sp) + (carry + r[:, :blk]))
        if mask is not None:
            p = jnp.where(mask, p, 0.0)
        acc_ref[...] += jnp.dot(p.astype(BF16), v, preferred_element_type=F32)
        carry_ref[...] = carry + r[:, blk:]

    carry_ref[...] = jnp.zeros_like(carry_ref)
    acc_ref[...] = jnp.zeros_like(acc_ref)
    row = lax.broadcasted_iota(jnp.int32, (blk, blk), 0)
    col = lax.broadcasted_iota(jnp.int32, (blk, blk), 1)
    step(i, col < row)

    def cond(j):
        return jnp.logical_and(j >= 0, jnp.max(carry_ref[...]) > SB_DEAD)

    def body(j):
        step(j, None)
        return j - 1

    lax.while_loop(cond, body, i - 1)
    o_ref[...] = (acc_ref[...] * _silu(g_ref[...])).astype(o_ref.dtype)


def _suffix_matrix(blk):
    j = np.arange(blk)[:, None]
    s = np.arange(blk)[None, :]
    half = np.concatenate([(j > s).astype(np.float32), np.ones((blk, blk), np.float32)], axis=1)
    return jnp.asarray(np.concatenate([half, half], axis=0), dtype=BF16)


def _stick_breaking(p_mm, p_ew):
    s = p_mm.shape[0]
    blk = SB_BLOCK
    return pl.pallas_call(
        _sb_kernel,
        grid=(N_HEADS, s // blk),
        in_specs=[
            pl.BlockSpec((2 * blk, 2 * blk), lambda h, i: (0, 0)),
            pl.BlockSpec((blk, HEAD_DIM), lambda h, i: (i, MM_Q * N_HEADS + h)),
            pl.BlockSpec((s, HEAD_DIM), lambda h, i: (0, MM_K * N_HEADS + h)),
            pl.BlockSpec((s, HEAD_DIM), lambda h, i: (0, MM_V * N_HEADS + h)),
            pl.BlockSpec((blk, HEAD_DIM), lambda h, i: (i, EW_AG * N_HEADS + h)),
        ],
        out_specs=pl.BlockSpec((blk, HEAD_DIM), lambda h, i: (i, h)),
        out_shape=jax.ShapeDtypeStruct((s, WIDTH), BF16),
        scratch_shapes=[pltpu.VMEM((blk, blk), F32), pltpu.VMEM((blk, HEAD_DIM), F32)],
        compiler_params=_cparams("parallel", "arbitrary"),
        name="sb_attn",
    )(_suffix_matrix(blk), p_mm, p_mm, p_mm, p_ew)


CONV_ROWS = 16


def _conv_kernel(a_ref, b_ref, pa_ref, pb_ref, g_ref, w_ref, cb_ref, lng_ref, lnb_ref, o_ref, gs_ref, *, ts):
    i = pl.program_id(0)
    halo = pa_ref[...] * _sigmoid(pb_ref[...])
    gs_ref[0:CONV_HALO, :] = jnp.where(i > 0, halo, 0.0)
    gs_ref[CONV_HALO:CONV_HALO + ts, :] = a_ref[...] * _sigmoid(b_ref[...])
    first = CONV_HALO - (CONV_KERNEL - 1)
    bias = cb_ref[...]
    lng = lng_ref[...]
    lnb = lnb_ref[...]
    for c in range(ts // CONV_ROWS):
        base = c * CONV_ROWS
        acc = gs_ref[base + first:base + first + CONV_ROWS, :] * w_ref[0:1, :]
        for k in range(1, CONV_KERNEL):
            acc = acc + gs_ref[base + first + k:base + first + k + CONV_ROWS, :] * w_ref[k:k + 1, :]
        y = _silu(_layernorm_rows(acc + bias, lng, lnb))
        out = y * _silu(g_ref[base:base + CONV_ROWS, :])
        o_ref[base:base + CONV_ROWS, :] = out.astype(o_ref.dtype)


def _conformer_conv(p_ew, conv_w, conv_b, ln_g, ln_b, layer, *, ts=256):
    s = p_ew.shape[0]
    hb = ts // CONV_HALO
    row = lambda i: (i, 0)
    par = lambda i: (layer, 0, 0)
    return pl.pallas_call(
        functools.partial(_conv_kernel, ts=ts),
        grid=(s // ts,),
        in_specs=[
            pl.BlockSpec((ts, WIDTH), lambda i: (i, EW_BA)),
            pl.BlockSpec((ts, WIDTH), lambda i: (i, EW_BB)),
            pl.BlockSpec((CONV_HALO, WIDTH), lambda i: (jnp.maximum(i * hb - 1, 0), EW_BA)),
            pl.BlockSpec((CONV_HALO, WIDTH), lambda i: (jnp.maximum(i * hb - 1, 0), EW_BB)),
            pl.BlockSpec((ts, WIDTH), lambda i: (i, EW_BG)),
            pl.BlockSpec((None, CONV_KERNEL, WIDTH), par),
            pl.BlockSpec((None, 1, WIDTH), par),
            pl.BlockSpec((None, 1, WIDTH), par),
            pl.BlockSpec((None, 1, WIDTH), par),
        ],
        out_specs=pl.BlockSpec((ts, WIDTH), row),
        out_shape=jax.ShapeDtypeStruct((s, WIDTH), BF16),
        scratch_shapes=[pltpu.VMEM((CONV_HALO + ts, WIDTH), F32)],
        compiler_params=_cparams("parallel"),
        name="conformer_conv",
    )(p_ew, p_ew, p_ew, p_ew, p_ew, conv_w, conv_b, ln_g, ln_b)


def _sgu_kernel(u_ref, v_ref, g_ref, w_ref, bs_ref, lng_ref, lnb_ref, o_ref, *, ts):
    ck = SGU_CHUNK
    gw = WIDTH // SGU_GROUPS
    row = lax.broadcasted_iota(jnp.int32, (ck, ck), 0)
    col = lax.broadcasted_iota(jnp.int32, (ck, ck), 1)
    tril = col <= row
    lng = lng_ref[...]
    lnb = lnb_ref[...]
    for c in range(ts // ck):
        rows = slice(c * ck, (c + 1) * ck)
        vn = _layernorm_rows(_gelu_tanh(v_ref[rows, :]), lng, lnb).astype(BF16)
        for g in range(SGU_GROUPS):
            cols = slice(g * gw, (g + 1) * gw)
            w = jnp.where(tril, w_ref[g], 0.0).astype(BF16)
            z = jnp.dot(w, vn[:, cols], preferred_element_type=F32) + bs_ref[:, cols]
            out = _gelu_tanh(u_ref[rows, cols]) * z * _silu(g_ref[rows, cols])
            o_ref[rows, cols] = out.astype(o_ref.dtype)


def _spatial_gating(p_ew, sgu_w, bs_full, ln_g, ln_b, layer, *, ts=256):
    s = p_ew.shape[0]
    par = lambda i: (layer, 0, 0)
    return pl.pallas_call(
        functools.partial(_sgu_kernel, ts=ts),
        grid=(s // ts,),
        in_specs=[
            pl.BlockSpec((ts, WIDTH), lambda i: (i, EW_CU)),
            pl.BlockSpec((ts, WIDTH), lambda i: (i, EW_CV)),
            pl.BlockSpec((ts, WIDTH), lambda i: (i, EW_CG)),
            pl.BlockSpec((None, SGU_GROUPS, SGU_CHUNK, SGU_CHUNK), lambda i: (layer, 0, 0, 0)),
            pl.BlockSpec((None, SGU_CHUNK, WIDTH), par),
            pl.BlockSpec((None, 1, WIDTH), par),
            pl.BlockSpec((None, 1, WIDTH), par),
        ],
        out_specs=pl.BlockSpec((ts, WIDTH), lambda i: (i, 0)),
        out_shape=jax.ShapeDtypeStruct((s, WIDTH), BF16),
        compiler_params=_cparams("parallel"),
        name="spatial_gating",
    )(p_ew, p_ew, p_ew, sgu_w, bs_full, ln_g, ln_b)


NEG_BIG = -1e30


def _dil_kernel(q_ref, kc_ref, kp_ref, vc_ref, vp_ref, o_ref, l_ref):
    t = DIL_BLOCK
    b = pl.program_id(1)
    row = lax.broadcasted_iota(jnp.int32, (t, t), 0)
    col = lax.broadcasted_iota(jnp.int32, (t, t), 1)
    mask_c = col <= row
    mask_p = jnp.logical_and(col >= row, b > 0)
    scale = HEAD_DIM ** -0.5
    dn = (((1,), (1,)), ((), ()))
    for h in range(N_HEADS):
        cols = slice(h * HEAD_DIM, (h + 1) * HEAD_DIM)
        q = q_ref[:, cols]
        s_c = lax.dot_general(q, kc_ref[:, cols], dn, preferred_element_type=F32) * scale
        s_p = lax.dot_general(q, kp_ref[:, cols], dn, preferred_element_type=F32) * scale
        s_c = jnp.where(mask_c, s_c, NEG_BIG)
        s_p = jnp.where(mask_p, s_p, NEG_BIG)
        m = jnp.maximum(jnp.max(s_c, axis=-1, keepdims=True), jnp.max(s_p, axis=-1, keepdims=True))
        p_c = jnp.exp(s_c - m)
        p_p = jnp.exp(s_p - m)
        den = jnp.sum(p_c, axis=-1, keepdims=True) + jnp.sum(p_p, axis=-1, keepdims=True)
        o = (jnp.dot(p_c.astype(BF16), vc_ref[:, cols], preferred_element_type=F32)
             + jnp.dot(p_p.astype(BF16), vp_ref[:, cols], preferred_element_type=F32))
        o_ref[:, cols] = o / den
        l_ref[:, cols] = jnp.broadcast_to(m + jnp.log(den), (t, HEAD_DIM))


def _dilated_group(p_mm, group):
    s = p_mm.shape[0]
    window, dil = DIL_PATTERNS[group]
    assert window // dil == DIL_BLOCK
    t = DIL_BLOCK
    l = s // dil
    nblk = MM_WIDTH // WIDTH
    pv = p_mm.reshape(l, dil * MM_WIDTH)
    cur = lambda off: (lambda r, b: (b, r * nblk + off))
    prev = lambda off: (lambda r, b: (jnp.maximum(b - 1, 0), r * nblk + off))
    blk = (t, WIDTH)
    out_spec = pl.BlockSpec(blk, lambda r, b: (b, r))
    shape = jax.ShapeDtypeStruct((l, dil * WIDTH), F32)
    o, lse = pl.pallas_call(
        _dil_kernel,
        grid=(dil, l // t),
        in_specs=[
            pl.BlockSpec(blk, cur(MM_DQ + group)),
            pl.BlockSpec(blk, cur(MM_DK + group)),
            pl.BlockSpec(blk, prev(MM_DK + group)),
            pl.BlockSpec(blk, cur(MM_DV)),
            pl.BlockSpec(blk, prev(MM_DV)),
        ],
        out_specs=[out_spec, out_spec],
        out_shape=[shape, shape],
        compiler_params=_cparams("parallel", "parallel"),
        name=f"dilated_g{group}",
    )(pv, pv, pv, pv, pv)
    return o.reshape(s, WIDTH), lse.reshape(s, WIDTH)


def _dil_merge_kernel(o0, o1, o2, l0, l1, l2, g_ref, out_ref):
    la, lb, lc = l0[...], l1[...], l2[...]
    m = jnp.maximum(jnp.maximum(la, lb), lc)
    ea, eb, ec = jnp.exp(la - m), jnp.exp(lb - m), jnp.exp(lc - m)
    den = ea + eb + ec
    y = (ea / den) * o0[...] + (eb / den) * o1[...] + (ec / den) * o2[...]
    out_ref[...] = (y * _silu(g_ref[...])).astype(out_ref.dtype)


def _dilated_mixture(p_mm, p_ew, *, ts=256):
    s = p_mm.shape[0]
    parts = [_dilated_group(p_mm, g) for g in range(len(DIL_PATTERNS))]
    spec = pl.BlockSpec((ts, WIDTH), lambda i: (i, 0))
    return pl.pallas_call(
        _dil_merge_kernel,
        grid=(s // ts,),
        in_specs=[spec] * 6 + [pl.BlockSpec((ts, WIDTH), lambda i: (i, EW_DG))],
        out_specs=spec,
        out_shape=jax.ShapeDtypeStruct((s, WIDTH), BF16),
        compiler_params=_cparams("parallel"),
        name="dilated_merge",
    )(parts[0][0], parts[1][0], parts[2][0], parts[0][1], parts[1][1], parts[2][1], p_ew)


def _merge_kernel(x_ref, ng_ref, ya_ref, yb_ref, yc_ref, yd_ref, wg0_ref, wg1_ref, wg2_ref, wg3_ref,
                  bg_ref, wb_ref, wo_ref, o_ref, h_ref, *, tm):
    @pl.when(pl.program_id(1) == 0)
    def _():
        _norm_into(x_ref, ng_ref, h_ref, tm)
        o_ref[...] = x_ref[...]

    h = h_ref[...]
    merged = None
    branches = ((ya_ref, wg0_ref), (yb_ref, wg1_ref), (yc_ref, wg2_ref), (yd_ref, wg3_ref))
    for n, (y_ref, wg_ref) in enumerate(branches):
        gate = _sigmoid(jnp.dot(h, wg_ref[...], preferred_element_type=F32) + bg_ref[n:n + 1, :])
        term = gate * jnp.dot(y_ref[...], wb_ref[n], preferred_element_type=F32)
        merged = term if merged is None else merged + term
    o_ref[...] += jnp.dot(merged.astype(BF16), wo_ref[...], preferred_element_type=F32)


def _merge(x, norm_g, ys, w_gate, b_gate, w_branch, w_out, layer, *, tm=512, tn=256):
    s, d = x.shape
    nj = d // tn
    y_spec = pl.BlockSpec((tm, WIDTH), lambda i, j: (i, 0))
    wg_spec = lambda n: pl.BlockSpec((None, d, tn), lambda i, j: (layer, 0, n * nj + j))
    return pl.pallas_call(
        functools.partial(_merge_kernel, tm=tm),
        grid=(s // tm, nj),
        in_specs=[
            pl.BlockSpec((tm, d), lambda i, j: (i, 0)),
            pl.BlockSpec((None, 1, d), lambda i, j: (layer, 0, 0)),
            y_spec, y_spec, y_spec, y_spec,
            wg_spec(0), wg_spec(1), wg_spec(2), wg_spec(3),
            pl.BlockSpec((None, N_BRANCH, tn), lambda i, j: (layer, 0, j)),
            pl.BlockSpec((None, N_BRANCH, WIDTH, tn), lambda i, j: (layer, 0, 0, j)),
            pl.BlockSpec((None, tn, d), lambda i, j: (layer, j, 0)),
        ],
        out_specs=pl.BlockSpec((tm, d), lambda i, j: (i, 0)),
        out_shape=jax.ShapeDtypeStruct((s, d), F32),
        scratch_shapes=[pltpu.VMEM((tm, d), BF16)],
        compiler_params=_cparams("parallel", "arbitrary"),
        name="merge_out",
    )(x, norm_g, *ys, w_gate, w_gate, w_gate, w_gate, b_gate, w_branch, w_out)


def _final_norm_kernel(x_ref, g_ref, o_ref):
    o_ref[...] = _rmsnorm_rows(x_ref[...], g_ref[...])


def _final_norm(x, g, *, ts=256):
    s, d = x.shape
    return pl.pallas_call(
        _final_norm_kernel,
        grid=(s // ts,),
        in_specs=[pl.BlockSpec((ts, d), lambda i: (i, 0)), pl.BlockSpec((1, d), lambda i: (0, 0))],
        out_specs=pl.BlockSpec((ts, d), lambda i: (i, 0)),
        out_shape=jax.ShapeDtypeStruct((s, d), F32),
        compiler_params=_cparams("parallel"),
        name="final_norm",
    )(x, g)


def kernel(x, norm_g, w_in, conv_w, conv_b, conv_ln_g, conv_ln_b, sgu_ln_g, sgu_ln_b, sgu_w, sgu_b, w_branch, w_gate, b_gate, w_out, final_g):
    batch, s, d = x.shape
    assert batch == 1 and d == D_MODEL
    depth = norm_g.shape[0]
    w_in_b = w_in.astype(BF16)
    w_gate_b = w_gate.astype(BF16)
    w_branch_b = w_branch.astype(BF16)
    w_out_b = w_out.astype(BF16)
    as_rows = lambda a: a.reshape(depth, 1, a.shape[-1])
    norm_g3, conv_b3 = as_rows(norm_g), as_rows(conv_b)
    conv_ln_g3, conv_ln_b3 = as_rows(conv_ln_g), as_rows(conv_ln_b)
    sgu_ln_g3, sgu_ln_b3 = as_rows(sgu_ln_g), as_rows(sgu_ln_b)
    b_gate3 = b_gate.reshape(depth, N_BRANCH, d)
    bs_full = jnp.repeat(jnp.swapaxes(sgu_b, 1, 2), WIDTH // SGU_GROUPS, axis=2)

    xs = x.reshape(s, d)
    for l in range(depth):
        p_mm = _project(xs, norm_g3, w_in_b, l, _MM_RANGES, BF16)
        p_ew = _project(xs, norm_g3, w_in_b, l, _EW_RANGES, F32)
        ya = _stick_breaking(p_mm, p_ew)
        yb = _conformer_conv(p_ew, conv_w, conv_b3, conv_ln_g3, conv_ln_b3, l)
        yc = _spatial_gating(p_ew, sgu_w, bs_full, sgu_ln_g3, sgu_ln_b3, l)
        yd = _dilated_mixture(p_mm, p_ew)
        xs = _merge(xs, norm_g3, (ya, yb, yc, yd), w_gate_b, b_gate3, w_branch_b, w_out_b, l)
    return _final_norm(xs, final_g.reshape(1, d)).reshape(batch, s, d)
```

```python
import functools

import numpy as np
import jax
import jax.numpy as jnp
from jax import lax
from jax.experimental import pallas as pl
from jax.experimental.pallas import tpu as pltpu

F32 = jnp.float32
BF16 = jnp.bfloat16

D_MODEL = 2048
HEAD_DIM = 128
NORM_EPS = 1e-6
N_HEADS = 8
WIDTH = N_HEADS * HEAD_DIM
CONV_KERNEL = 31
CONV_HALO = 32
SGU_GROUPS = 8
SGU_CHUNK = 128
DIL_PATTERNS = ((128, 1), (512, 4), (2048, 16))
DIL_BLOCK = 128
N_BRANCH = 4
IN_WIDTH = 18432

MM_WIDTH = 3072
EW_WIDTH = IN_WIDTH - MM_WIDTH
MM_Q, MM_K, MM_V = 0, 1, 2
EW_AG, EW_BA, EW_BB, EW_BG, EW_CU, EW_CV, EW_CG, EW_DQ, EW_DK, EW_DV, EW_DG = 0, 1, 2, 3, 4, 5, 6, 7, 10, 13, 14

SB_DEAD = -104.0
NEG_BIG = -1e30
VMEM_LIMIT = 56 * 1024 * 1024


def _cparams(*sem):
    return pltpu.CompilerParams(dimension_semantics=sem, vmem_limit_bytes=VMEM_LIMIT)


def _sigmoid(x):
    return 1.0 / (1.0 + jnp.exp(-x))


def _silu(x):
    return x * _sigmoid(x)


def _gelu_tanh(x):
    c = np.sqrt(2.0 / np.pi).astype(np.float32)
    return x * (0.5 * (1.0 + jnp.tanh(c * (x + 0.044715 * (x * x * x)))))


def _rmsnorm_rows(x, g):
    ms = jnp.mean(x * x, axis=-1, keepdims=True)
    return x * lax.rsqrt(ms + NORM_EPS) * g


def _layernorm_rows(x, g, b):
    mu = jnp.mean(x, axis=-1, keepdims=True)
    xc = x - mu
    var = jnp.mean(xc * xc, axis=-1, keepdims=True)
    return xc * lax.rsqrt(var + NORM_EPS) * g + b


NORM_ROWS = 256


def _norm_into(x_ref, g_ref, h_ref, rows):
    g = g_ref[...]
    for c in range(rows // NORM_ROWS):
        sl = slice(c * NORM_ROWS, (c + 1) * NORM_ROWS)
        h_ref[sl, :] = _rmsnorm_rows(x_ref[sl, :], g).astype(h_ref.dtype)


def _proj_kernel(x_ref, g_ref, w_ref, mm_ref, ew_ref, h_ref, *, tm, n_mm):
    j = pl.program_id(1)

    @pl.when(j == 0)
    def _():
        _norm_into(x_ref, g_ref, h_ref, tm)

    res = jnp.dot(h_ref[...], w_ref[...], preferred_element_type=F32)

    @pl.when(j < n_mm)
    def _():
        mm_ref[...] = res.astype(mm_ref.dtype)

    @pl.when(j >= n_mm)
    def _():
        ew_ref[...] = res


def _project(x, norm_g, w_in, layer, *, tm=1024, tn=512):
    s, d = x.shape
    n_mm = MM_WIDTH // tn
    n_tiles = IN_WIDTH // tn
    return pl.pallas_call(
        functools.partial(_proj_kernel, tm=tm, n_mm=n_mm),
        grid=(s // tm, n_tiles),
        in_specs=[
            pl.BlockSpec((tm, d), lambda i, j: (i, 0)),
            pl.BlockSpec((None, 1, d), lambda i, j: (layer, 0, 0)),
            pl.BlockSpec((None, d, tn), lambda i, j: (layer, 0, j)),
        ],
        out_specs=[
            pl.BlockSpec((tm, tn), lambda i, j: (i, jnp.minimum(j, n_mm - 1))),
            pl.BlockSpec((tm, tn), lambda i, j: (i, jnp.maximum(j - n_mm, 0))),
        ],
        out_shape=[jax.ShapeDtypeStruct((s, MM_WIDTH), BF16), jax.ShapeDtypeStruct((s, EW_WIDTH), F32)],
        scratch_shapes=[pltpu.VMEM((tm, d), BF16)],
        compiler_params=_cparams("parallel", "arbitrary"),
        name="in_proj",
    )(x, norm_g, w_in)


SB_BLOCK = 128
SB_ROWS = 256
SB_HEADS = 4


def _sb_kernel(uu_ref, q_ref, k_ref, v_ref, g_ref, o_ref, carry_ref, acc_ref):
    blk, tq = SB_BLOCK, SB_ROWS
    nsub = tq // blk
    i = pl.program_id(1)
    uu = uu_ref[...]
    scale = HEAD_DIM ** -0.5

    def step(j, r0, masked):
        start = pl.multiple_of(j * blk, blk)
        rows = slice(r0, tq)
        if masked:
            row = lax.broadcasted_iota(jnp.int32, (tq - r0, blk), 0)
            col = lax.broadcasted_iota(jnp.int32, (tq - r0, blk), 1)
            mask = col < row
        alive = None
        for h in range(SB_HEADS):
            cols = slice(h * HEAD_DIM, (h + 1) * HEAD_DIM)
            k = k_ref[pl.ds(start, blk), cols]
            v = v_ref[pl.ds(start, blk), cols]
            z = lax.dot_general(q_ref[rows, cols], k, (((1,), (1,)), ((), ())), preferred_element_type=F32) * scale
            sp = jnp.maximum(z, 0.0) + jnp.log1p(jnp.exp(-jnp.abs(z)))
            l1mb = -sp
            if masked:
                l1mb = jnp.where(mask, l1mb, 0.0)
            hi = l1mb.astype(BF16)
            lo = (l1mb - hi.astype(F32)).astype(BF16)
            r = jnp.dot(jnp.concatenate([hi, lo], axis=1), uu, preferred_element_type=F32)
            carry = carry_ref[h, rows, :]
            p = jnp.exp((z - sp) + (carry + r[:, :blk]))
            if masked:
                p = jnp.where(mask, p, 0.0)
            acc_ref[rows, cols] += jnp.dot(p.astype(BF16), v, preferred_element_type=F32)
            carry = carry + r[:, blk:]
            carry_ref[h, rows, :] = carry
            top = jnp.max(carry)
            alive = top if alive is None else jnp.maximum(alive, top)
        return alive

    carry_ref[...] = jnp.zeros_like(carry_ref)
    acc_ref[...] = jnp.zeros_like(acc_ref)
    for c in reversed(range(nsub)):
        alive = step(i * nsub + c, c * blk, True)

    def cond(state):
        j, top = state
        return jnp.logical_and(j >= 0, top > SB_DEAD)

    def body(state):
        j, _ = state
        return j - 1, step(j, 0, False)

    lax.while_loop(cond, body, (i * nsub - 1, alive))
    o_ref[...] = (acc_ref[...] * _silu(g_ref[...])).astype(o_ref.dtype)


def _suffix_matrix(blk):
    j = np.arange(blk)[:, None]
    s = np.arange(blk)[None, :]
    half = np.concatenate([(j > s).astype(np.float32), np.ones((blk, blk), np.float32)], axis=1)
    return jnp.asarray(np.concatenate([half, half], axis=0), dtype=BF16)


def _stick_breaking(p_mm, p_ew):
    s = p_mm.shape[0]
    blk, tq, hw = SB_BLOCK, SB_ROWS, SB_HEADS * HEAD_DIM
    hb = WIDTH // hw
    return pl.pallas_call(
        _sb_kernel,
        grid=(hb, s // tq),
        in_specs=[
            pl.BlockSpec((2 * blk, 2 * blk), lambda h, i: (0, 0)),
            pl.BlockSpec((tq, hw), lambda h, i: (i, MM_Q * hb + h)),
            pl.BlockSpec((s, hw), lambda h, i: (0, MM_K * hb + h)),
            pl.BlockSpec((s, hw), lambda h, i: (0, MM_V * hb + h)),
            pl.BlockSpec((tq, hw), lambda h, i: (i, EW_AG * hb + h)),
        ],
        out_specs=pl.BlockSpec((tq, hw), lambda h, i: (i, h)),
        out_shape=jax.ShapeDtypeStruct((s, WIDTH), BF16),
        scratch_shapes=[pltpu.VMEM((SB_HEADS, tq, blk), F32), pltpu.VMEM((tq, hw), F32)],
        compiler_params=_cparams("parallel", "arbitrary"),
        name="sb_attn",
    )(_suffix_matrix(blk), p_mm, p_mm, p_mm, p_ew)


CONV_ROWS = 16
CONV_STAGE_ROWS = 64
CONV_SHIFT_ROWS = 40
SUBLANES = 8


def _conv_kernel(a_ref, b_ref, pa_ref, pb_ref, g_ref, w_ref, cb_ref, lng_ref, lnb_ref, o_ref, gs_ref, wb_ref, *, ts):
    i = pl.program_id(0)
    sub = SUBLANES
    halo = pa_ref[...] * _sigmoid(pb_ref[...])
    gs_ref[0, 0:CONV_HALO, :] = jnp.where(i > 0, halo, 0.0)
    for c in range(ts // CONV_STAGE_ROWS):
        rows = slice(c * CONV_STAGE_ROWS, (c + 1) * CONV_STAGE_ROWS)
        gs_ref[0, CONV_HALO + c * CONV_STAGE_ROWS:CONV_HALO + (c + 1) * CONV_STAGE_ROWS, :] = (
            a_ref[rows, :] * _sigmoid(b_ref[rows, :]))
    span = CONV_HALO + ts - sub
    for s in range(1, sub):
        for c in range(span // CONV_SHIFT_ROWS):
            r = c * CONV_SHIFT_ROWS
            gs_ref[s, r:r + CONV_SHIFT_ROWS, :] = gs_ref[0, r + s:r + s + CONV_SHIFT_ROWS, :]
    for k in range(CONV_KERNEL):
        wb_ref[k] = jnp.broadcast_to(w_ref[k:k + 1, :], (CONV_ROWS, WIDTH))
    first = CONV_HALO - (CONV_KERNEL - 1)
    bias = cb_ref[...]
    lng = lng_ref[...]
    lnb = lnb_ref[...]
    for c in range(ts // CONV_ROWS):
        base = c * CONV_ROWS
        acc = None
        for k in range(CONV_KERNEL):
            phase = (first + k) % sub
            r = base + first + k - phase
            term = gs_ref[phase, r:r + CONV_ROWS, :] * wb_ref[k]
            acc = term if acc is None else acc + term
        y = _silu(_layernorm_rows(acc + bias, lng, lnb))
        out = y * _silu(g_ref[base:base + CONV_ROWS, :])
        o_ref[base:base + CONV_ROWS, :] = out.astype(o_ref.dtype)


def _conformer_conv(p_ew, conv_w, conv_b, ln_g, ln_b, layer, *, ts=256):
    s = p_ew.shape[0]
    hb = ts // CONV_HALO
    row = lambda i: (i, 0)
    par = lambda i: (layer, 0, 0)
    return pl.pallas_call(
        functools.partial(_conv_kernel, ts=ts),
        grid=(s // ts,),
        in_specs=[
            pl.BlockSpec((ts, WIDTH), lambda i: (i, EW_BA)),
            pl.BlockSpec((ts, WIDTH), lambda i: (i, EW_BB)),
            pl.BlockSpec((CONV_HALO, WIDTH), lambda i: (jnp.maximum(i * hb - 1, 0), EW_BA)),
            pl.BlockSpec((CONV_HALO, WIDTH), lambda i: (jnp.maximum(i * hb - 1, 0), EW_BB)),
            pl.BlockSpec((ts, WIDTH), lambda i: (i, EW_BG)),
            pl.BlockSpec((None, CONV_KERNEL, WIDTH), par),
            pl.BlockSpec((None, 1, WIDTH), par),
            pl.BlockSpec((None, 1, WIDTH), par),
            pl.BlockSpec((None, 1, WIDTH), par),
        ],
        out_specs=pl.BlockSpec((ts, WIDTH), row),
        out_shape=jax.ShapeDtypeStruct((s, WIDTH), BF16),
        scratch_shapes=[pltpu.VMEM((SUBLANES, CONV_HALO + ts, WIDTH), F32),
                        pltpu.VMEM((CONV_KERNEL, CONV_ROWS, WIDTH), F32)],
        compiler_params=_cparams("parallel"),
        name="conformer_conv",
    )(p_ew, p_ew, p_ew, p_ew, p_ew, conv_w, conv_b, ln_g, ln_b)


def _sgu_kernel(u_ref, v_ref, g_ref, w_ref, bs_ref, lng_ref, lnb_ref, o_ref, *, ts):
    ck = SGU_CHUNK
    gw = WIDTH // SGU_GROUPS
    row = lax.broadcasted_iota(jnp.int32, (ck, ck), 0)
    col = lax.broadcasted_iota(jnp.int32, (ck, ck), 1)
    tril = col <= row
    lng = lng_ref[...]
    lnb = lnb_ref[...]
    ws = [jnp.where(tril, w_ref[g], 0.0).astype(BF16) for g in range(SGU_GROUPS)]
    for c in range(ts // ck):
        rows = slice(c * ck, (c + 1) * ck)
        vn = _layernorm_rows(_gelu_tanh(v_ref[rows, :]), lng, lnb).astype(BF16)
        for g in range(SGU_GROUPS):
            cols = slice(g * gw, (g + 1) * gw)
            z = jnp.dot(ws[g], vn[:, cols], preferred_element_type=F32) + bs_ref[:, cols]
            out = _gelu_tanh(u_ref[rows, cols]) * z * _silu(g_ref[rows, cols])
            o_ref[rows, cols] = out.astype(o_ref.dtype)


def _spatial_gating(p_ew, sgu_w, bs_full, ln_g, ln_b, layer, *, ts=256):
    s = p_ew.shape[0]
    par = lambda i: (layer, 0, 0)
    return pl.pallas_call(
        functools.partial(_sgu_kernel, ts=ts),
        grid=(s // ts,),
        in_specs=[
            pl.BlockSpec((ts, WIDTH), lambda i: (i, EW_CU)),
            pl.BlockSpec((ts, WIDTH), lambda i: (i, EW_CV)),
            pl.BlockSpec((ts, WIDTH), lambda i: (i, EW_CG)),
            pl.BlockSpec((None, SGU_GROUPS, SGU_CHUNK, SGU_CHUNK), lambda i: (layer, 0, 0, 0)),
            pl.BlockSpec((None, SGU_CHUNK, WIDTH), par),
            pl.BlockSpec((None, 1, WIDTH), par),
            pl.BlockSpec((None, 1, WIDTH), par),
        ],
        out_specs=pl.BlockSpec((ts, WIDTH), lambda i: (i, 0)),
        out_shape=jax.ShapeDtypeStruct((s, WIDTH), BF16),
        compiler_params=_cparams("parallel"),
        name="spatial_gating",
    )(p_ew, p_ew, p_ew, sgu_w, bs_full, ln_g, ln_b)


DIL_SUPER = 2048
DIL_COMBINE_ROWS = 256
DIL_UNROLL = 4


def _strided_rows(ref, start, size, stride):
    if stride == 1:
        return ref[start:start + size, :]
    return ref[pl.ds(start, size, stride=stride), :]


def _dil_kernel(q0_ref, q1_ref, q2_ref, k0_ref, k1_ref, k2_ref, kp0_ref, kp1_ref, kp2_ref, v_ref, vp_ref, g_ref,
                o_ref, qs_ref, ks_ref, vs_ref, num_ref, max_ref, den_ref, pos_ref):
    t, sup = DIL_BLOCK, DIL_SUPER
    i = pl.program_id(0)
    q_refs = (q0_ref, q1_ref, q2_ref)
    k_refs = (k0_ref, k1_ref, k2_ref)
    kp_refs = (kp0_ref, kp1_ref, kp2_ref)
    scale = HEAD_DIM ** -0.5
    row = lax.broadcasted_iota(jnp.int32, (t, 2 * t), 0)
    col = lax.broadcasted_iota(jnp.int32, (t, 2 * t), 1)
    band = jnp.logical_and(col >= row, col <= row + t)
    own = col >= t

    for g, (_, dil) in enumerate(DIL_PATTERNS):
        n = sup // dil
        for r in range(dil):
            qs_ref[g, r * n:(r + 1) * n, :] = _strided_rows(q_refs[g], r, n, dil).astype(BF16)
            base = r * (n + t)
            ks_ref[g, base:base + t, :] = _strided_rows(kp_refs[g], r, t, dil).astype(BF16)
            ks_ref[g, base + t:base + t + n, :] = _strided_rows(k_refs[g], r, n, dil).astype(BF16)
            vs_ref[g, base:base + t, :] = _strided_rows(vp_ref, sup - t * dil + r, t, dil).astype(BF16)
            vs_ref[g, base + t:base + t + n, :] = _strided_rows(v_ref, r, n, dil).astype(BF16)

        per_class = n // t

        def block(u, carry, g=g, per_class=per_class):
            r = u // per_class
            bb = u - r * per_class
            qoff = pl.multiple_of(u * t, t)
            koff = pl.multiple_of((u + r) * t, t)
            q = qs_ref[g, pl.ds(qoff, t), :]
            k = ks_ref[g, pl.ds(koff, 2 * t), :]
            v = vs_ref[g, pl.ds(koff, 2 * t), :]
            s = lax.dot_general(q, k, (((1,), (1,)), ((), ())), preferred_element_type=F32) * scale
            has_prev = jnp.logical_or(i > 0, bb > 0)
            valid = jnp.logical_and(band, jnp.logical_or(own, has_prev))
            s = jnp.where(valid, s, NEG_BIG)
            m = jnp.max(s, axis=-1, keepdims=True)
            p = jnp.exp(s - m)
            den = jnp.sum(p, axis=-1, keepdims=True)
            num_ref[g, pl.ds(qoff, t), :] = jnp.dot(p.astype(BF16), v, preferred_element_type=F32)
            max_ref[g, pl.ds(qoff, t), :] = jnp.broadcast_to(m, (t, HEAD_DIM))
            den_ref[g, pl.ds(qoff, t), :] = jnp.broadcast_to(den, (t, HEAD_DIM))
            return carry

        lax.fori_loop(0, sup // t, block, 0, unroll=DIL_UNROLL)

    def to_positions(src_ref, g, slot):
        dil = DIL_PATTERNS[g][1]
        n = sup // dil
        for r in range(dil):
            pos_ref[slot, pl.ds(r, n, stride=dil), :] = src_ref[g, r * n:(r + 1) * n, :]

    slot = 0
    for g in (1, 2):
        for src in (num_ref, max_ref, den_ref):
            to_positions(src, g, slot)
            slot += 1

    cr = DIL_COMBINE_ROWS
    for c in range(sup // cr):
        rows = slice(c * cr, (c + 1) * cr)
        m0, m1, m2 = max_ref[0, rows, :], pos_ref[1, rows, :], pos_ref[4, rows, :]
        m = jnp.maximum(jnp.maximum(m0, m1), m2)
        e0, e1, e2 = jnp.exp(m0 - m), jnp.exp(m1 - m), jnp.exp(m2 - m)
        num = e0 * num_ref[0, rows, :] + e1 * pos_ref[0, rows, :] + e2 * pos_ref[3, rows, :]
        den = e0 * den_ref[0, rows, :] + e1 * pos_ref[2, rows, :] + e2 * pos_ref[5, rows, :]
        o_ref[rows, :] = ((num / den) * _silu(g_ref[rows, :])).astype(o_ref.dtype)


def _dilated_mixture(p_ew):
    s = p_ew.shape[0]
    t, sup = DIL_BLOCK, DIL_SUPER
    assert all(w // d == t for w, d in DIL_PATTERNS) and sup == t * DIL_PATTERNS[-1][1]
    nh = N_HEADS
    cur = lambda blk: (lambda i, h: (i, blk * nh + h))
    specs = [pl.BlockSpec((sup, HEAD_DIM), cur(EW_DQ + g)) for g in range(3)]
    specs += [pl.BlockSpec((sup, HEAD_DIM), cur(EW_DK + g)) for g in range(3)]
    for g, (_, dil) in enumerate(DIL_PATTERNS):
        rows = t * dil
        per = sup // rows
        specs.append(pl.BlockSpec((rows, HEAD_DIM),
                                  lambda i, h, g=g, per=per: (jnp.maximum(i * per - 1, 0), (EW_DK + g) * nh + h)))
    specs.append(pl.BlockSpec((sup, HEAD_DIM), cur(EW_DV)))
    specs.append(pl.BlockSpec((sup, HEAD_DIM), lambda i, h: (jnp.maximum(i - 1, 0), EW_DV * nh + h)))
    specs.append(pl.BlockSpec((sup, HEAD_DIM), cur(EW_DG)))
    hist = max(sup + t * d for _, d in DIL_PATTERNS)
    return pl.pallas_call(
        _dil_kernel,
        grid=(s // sup, nh),
        in_specs=specs,
        out_specs=pl.BlockSpec((sup, HEAD_DIM), lambda i, h: (i, h)),
        out_shape=jax.ShapeDtypeStruct((s, WIDTH), BF16),
        scratch_shapes=[
            pltpu.VMEM((3, sup, HEAD_DIM), BF16),
            pltpu.VMEM((3, hist, HEAD_DIM), BF16),
            pltpu.VMEM((3, hist, HEAD_DIM), BF16),
            pltpu.VMEM((3, sup, HEAD_DIM), F32),
            pltpu.VMEM((3, sup, HEAD_DIM), F32),
            pltpu.VMEM((3, sup, HEAD_DIM), F32),
            pltpu.VMEM((6, sup, HEAD_DIM), F32),
        ],
        compiler_params=_cparams("parallel", "parallel"),
        name="dilated_attn",
    )(*([p_ew] * 12))


def _merge_kernel(x_ref, ng_ref, ya_ref, yb_ref, yc_ref, yd_ref, wg0_ref, wg1_ref, wg2_ref, wg3_ref,
                  bg_ref, wb_ref, wo_ref, o_ref, h_ref, *, tm):
    @pl.when(pl.program_id(1) == 0)
    def _():
        _norm_into(x_ref, ng_ref, h_ref, tm)
        o_ref[...] = x_ref[...]

    h = h_ref[...]
    merged = None
    branches = ((ya_ref, wg0_ref), (yb_ref, wg1_ref), (yc_ref, wg2_ref), (yd_ref, wg3_ref))
    for n, (y_ref, wg_ref) in enumerate(branches):
        gate = _sigmoid(jnp.dot(h, wg_ref[...], preferred_element_type=F32) + bg_ref[n:n + 1, :])
        term = gate * jnp.dot(y_ref[...], wb_ref[n], preferred_element_type=F32)
        merged = term if merged is None else merged + term
    o_ref[...] += jnp.dot(merged.astype(BF16), wo_ref[...], preferred_element_type=F32)


def _merge(x, norm_g, ys, w_gate, b_gate, w_branch, w_out, layer, *, tm=512, tn=256):
    s, d = x.shape
    nj = d // tn
    y_spec = pl.BlockSpec((tm, WIDTH), lambda i, j: (i, 0))
    wg_spec = lambda n: pl.BlockSpec((None, d, tn), lambda i, j: (layer, 0, n * nj + j))
    return pl.pallas_call(
        functools.partial(_merge_kernel, tm=tm),
        grid=(s // tm, nj),
        in_specs=[
            pl.BlockSpec((tm, d), lambda i, j: (i, 0)),
            pl.BlockSpec((None, 1, d), lambda i, j: (layer, 0, 0)),
            y_spec, y_spec, y_spec, y_spec,
            wg_spec(0), wg_spec(1), wg_spec(2), wg_spec(3),
            pl.BlockSpec((None, N_BRANCH, tn), lambda i, j: (layer, 0, j)),
            pl.BlockSpec((None, N_BRANCH, WIDTH, tn), lambda i, j: (layer, 0, 0, j)),
            pl.BlockSpec((None, tn, d), lambda i, j: (layer, j, 0)),
        ],
        out_specs=pl.BlockSpec((tm, d), lambda i, j: (i, 0)),
        out_shape=jax.ShapeDtypeStruct((s, d), F32),
        scratch_shapes=[pltpu.VMEM((tm, d), BF16)],
        compiler_params=_cparams("parallel", "arbitrary"),
        name="merge_out",
    )(x, norm_g, *ys, w_gate, w_gate, w_gate, w_gate, b_gate, w_branch, w_out)


def _final_norm_kernel(x_ref, g_ref, o_ref):
    o_ref[...] = _rmsnorm_rows(x_ref[...], g_ref[...])


def _final_norm(x, g, *, ts=256):
    s, d = x.shape
    return pl.pallas_call(
        _final_norm_kernel,
        grid=(s // ts,),
        in_specs=[pl.BlockSpec((ts, d), lambda i: (i, 0)), pl.BlockSpec((1, d), lambda i: (0, 0))],
        out_specs=pl.BlockSpec((ts, d), lambda i: (i, 0)),
        out_shape=jax.ShapeDtypeStruct((s, d), F32),
        compiler_params=_cparams("parallel"),
        name="final_norm",
    )(x, g)


def kernel(x, norm_g, w_in, conv_w, conv_b, conv_ln_g, conv_ln_b, sgu_ln_g, sgu_ln_b, sgu_w, sgu_b, w_branch, w_gate, b_gate, w_out, final_g):
    batch, s, d = x.shape
    assert batch == 1 and d == D_MODEL
    depth = norm_g.shape[0]
    w_in_b = w_in.astype(BF16)
    w_gate_b = w_gate.astype(BF16)
    w_branch_b = w_branch.astype(BF16)
    w_out_b = w_out.astype(BF16)
    as_rows = lambda a: a.reshape(depth, 1, a.shape[-1])
    norm_g3, conv_b3 = as_rows(norm_g), as_rows(conv_b)
    conv_ln_g3, conv_ln_b3 = as_rows(conv_ln_g), as_rows(conv_ln_b)
    sgu_ln_g3, sgu_ln_b3 = as_rows(sgu_ln_g), as_rows(sgu_ln_b)
    b_gate3 = b_gate.reshape(depth, N_BRANCH, d)
    bs_full = jnp.repeat(jnp.swapaxes(sgu_b, 1, 2), WIDTH // SGU_GROUPS, axis=2)

    xs = x.reshape(s, d)
    for l in range(depth):
        p_mm, p_ew = _project(xs, norm_g3, w_in_b, l)
        ya = _stick_breaking(p_mm, p_ew)
        yb = _conformer_conv(p_ew, conv_w, conv_b3, conv_ln_g3, conv_ln_b3, l)
        yc = _spatial_gating(p_ew, sgu_w, bs_full, sgu_ln_g3, sgu_ln_b3, l)
        yd = _dilated_mixture(p_ew)
        xs = _merge(xs, norm_g3, (ya, yb, yc, yd), w_gate_b, b_gate3, w_branch_b, w_out_b, l)
    return _final_norm(xs, final_g.reshape(1, d)).reshape(batch, s, d)
```

```python
import functools

import numpy as np
import jax
import jax.numpy as jnp
from jax import lax
from jax.experimental import pallas as pl
from jax.experimental.pallas import tpu as pltpu

F32 = jnp.float32
BF16 = jnp.bfloat16

D_MODEL = 2048
HEAD_DIM = 128
NORM_EPS = 1e-6
N_HEADS = 8
WIDTH = N_HEADS * HEAD_DIM
CONV_KERNEL = 31
CONV_HALO = 32
SGU_GROUPS = 8
SGU_CHUNK = 128
DIL_PATTERNS = ((128, 1), (512, 4), (2048, 16))
DIL_BLOCK = 128
N_BRANCH = 4
IN_WIDTH = 18432

MM_WIDTH = 3072
EW_WIDTH = IN_WIDTH - MM_WIDTH
MM_Q, MM_K, MM_V = 0, 1, 2
EW_AG, EW_BA, EW_BB, EW_BG, EW_CU, EW_CV, EW_CG, EW_DQ, EW_DK, EW_DV, EW_DG = 0, 1, 2, 3, 4, 5, 6, 7, 10, 13, 14

SB_DEAD = -104.0
NEG_BIG = -1e30
VMEM_LIMIT = 56 * 1024 * 1024


def _cparams(*sem):
    return pltpu.CompilerParams(dimension_semantics=sem, vmem_limit_bytes=VMEM_LIMIT)


def _sigmoid(x):
    return 1.0 / (1.0 + jnp.exp(-x))


def _silu(x):
    return x * _sigmoid(x)


def _gelu_tanh(x):
    c = np.sqrt(2.0 / np.pi).astype(np.float32)
    return x * (0.5 * (1.0 + jnp.tanh(c * (x + 0.044715 * (x * x * x)))))


def _rmsnorm_rows(x, g):
    ms = jnp.mean(x * x, axis=-1, keepdims=True)
    return x * lax.rsqrt(ms + NORM_EPS) * g


def _layernorm_rows(x, g, b):
    mu = jnp.mean(x, axis=-1, keepdims=True)
    xc = x - mu
    var = jnp.mean(xc * xc, axis=-1, keepdims=True)
    return xc * lax.rsqrt(var + NORM_EPS) * g + b


NORM_ROWS = 256


def _norm_into(x_ref, g_ref, h_ref, rows):
    g = g_ref[...]
    for c in range(rows // NORM_ROWS):
        sl = slice(c * NORM_ROWS, (c + 1) * NORM_ROWS)
        h_ref[sl, :] = _rmsnorm_rows(x_ref[sl, :], g).astype(h_ref.dtype)


def _proj_kernel(x_ref, g_ref, w_ref, mm_ref, ew_ref, h_ref, *, tm, n_mm):
    j = pl.program_id(1)

    @pl.when(j == 0)
    def _():
        _norm_into(x_ref, g_ref, h_ref, tm)

    res = jnp.dot(h_ref[...], w_ref[...], preferred_element_type=F32)

    @pl.when(j < n_mm)
    def _():
        mm_ref[...] = res.astype(mm_ref.dtype)

    @pl.when(j >= n_mm)
    def _():
        ew_ref[...] = res


def _project(x, norm_g, w_in, layer, *, tm=1024, tn=1024):
    s, d = x.shape
    n_mm = MM_WIDTH // tn
    n_tiles = IN_WIDTH // tn
    return pl.pallas_call(
        functools.partial(_proj_kernel, tm=tm, n_mm=n_mm),
        grid=(s // tm, n_tiles),
        in_specs=[
            pl.BlockSpec((tm, d), lambda i, j: (i, 0)),
            pl.BlockSpec((None, 1, d), lambda i, j: (layer, 0, 0)),
            pl.BlockSpec((None, d, tn), lambda i, j: (layer, 0, j)),
        ],
        out_specs=[
            pl.BlockSpec((tm, tn), lambda i, j: (i, jnp.minimum(j, n_mm - 1))),
            pl.BlockSpec((tm, tn), lambda i, j: (i, jnp.maximum(j - n_mm, 0))),
        ],
        out_shape=[jax.ShapeDtypeStruct((s, MM_WIDTH), BF16), jax.ShapeDtypeStruct((s, EW_WIDTH), F32)],
        scratch_shapes=[pltpu.VMEM((tm, d), BF16)],
        compiler_params=_cparams("parallel", "arbitrary"),
        name="in_proj",
    )(x, norm_g, w_in)


SB_BLOCK = 128
SB_ROWS = 256
SB_HEADS = 4


def _sb_kernel(uu_ref, q_ref, k_ref, v_ref, g_ref, o_ref, carry_ref, acc_ref):
    blk, tq = SB_BLOCK, SB_ROWS
    nsub = tq // blk
    i = pl.program_id(1)
    uu = uu_ref[...]
    scale = HEAD_DIM ** -0.5

    def step(j, r0, masked):
        start = pl.multiple_of(j * blk, blk)
        rows = slice(r0, tq)
        if masked:
            row = lax.broadcasted_iota(jnp.int32, (tq - r0, blk), 0)
            col = lax.broadcasted_iota(jnp.int32, (tq - r0, blk), 1)
            mask = col < row
        heads = range(SB_HEADS)
        cols = [slice(h * HEAD_DIM, (h + 1) * HEAD_DIM) for h in heads]
        carries = [carry_ref[h, rows, :] for h in heads]
        accs = [acc_ref[rows, cols[h]] for h in heads]
        nt = (((1,), (1,)), ((), ()))
        zs = [lax.dot_general(q_ref[rows, cols[h]], k_ref[pl.ds(start, blk), cols[h]], nt,
                              preferred_element_type=F32) * scale for h in heads]
        lbetas, splits = [], []
        for h in heads:
            sp = jnp.maximum(zs[h], 0.0) + jnp.log(1.0 + jnp.exp(-jnp.abs(zs[h])))
            l1mb = -sp
            if masked:
                l1mb = jnp.where(mask, l1mb, 0.0)
            hi = l1mb.astype(BF16)
            lo = (l1mb - hi.astype(F32)).astype(BF16)
            lbetas.append(zs[h] - sp)
            splits.append(jnp.concatenate([hi, lo], axis=1))
        rs = [jnp.dot(splits[h], uu, preferred_element_type=F32) for h in heads]
        ps = []
        for h in heads:
            p = jnp.exp(lbetas[h] + (carries[h] + rs[h][:, :blk]))
            if masked:
                p = jnp.where(mask, p, 0.0)
            ps.append(p.astype(BF16))
            carries[h] = carries[h] + rs[h][:, blk:]
        for h in heads:
            accs[h] = accs[h] + jnp.dot(ps[h], v_ref[pl.ds(start, blk), cols[h]], preferred_element_type=F32)
        for h in heads:
            carry_ref[h, rows, :] = carries[h]
            acc_ref[rows, cols[h]] = accs[h]
        top = functools.reduce(jnp.maximum, carries)
        return jnp.max(top)

    carry_ref[...] = jnp.zeros_like(carry_ref)
    acc_ref[...] = jnp.zeros_like(acc_ref)
    for c in reversed(range(nsub)):
        alive = step(i * nsub + c, c * blk, True)

    def cond(state):
        j, top = state
        return jnp.logical_and(j >= 0, top > SB_DEAD)

    def body(state):
        j, _ = state
        return j - 1, step(j, 0, False)

    lax.while_loop(cond, body, (i * nsub - 1, alive))
    o_ref[...] = (acc_ref[...] * _silu(g_ref[...])).astype(o_ref.dtype)


def _suffix_matrix(blk):
    j = np.arange(blk)[:, None]
    s = np.arange(blk)[None, :]
    half = np.concatenate([(j > s).astype(np.float32), np.ones((blk, blk), np.float32)], axis=1)
    return jnp.asarray(np.concatenate([half, half], axis=0), dtype=BF16)


def _stick_breaking(p_mm, p_ew):
    s = p_mm.shape[0]
    blk, tq, hw = SB_BLOCK, SB_ROWS, SB_HEADS * HEAD_DIM
    hb = WIDTH // hw
    return pl.pallas_call(
        _sb_kernel,
        grid=(hb, s // tq),
        in_specs=[
            pl.BlockSpec((2 * blk, 2 * blk), lambda h, i: (0, 0)),
            pl.BlockSpec((tq, hw), lambda h, i: (i, MM_Q * hb + h)),
            pl.BlockSpec((s, hw), lambda h, i: (0, MM_K * hb + h)),
            pl.BlockSpec((s, hw), lambda h, i: (0, MM_V * hb + h)),
            pl.BlockSpec((tq, hw), lambda h, i: (i, EW_AG * hb + h)),
        ],
        out_specs=pl.BlockSpec((tq, hw), lambda h, i: (i, h)),
        out_shape=jax.ShapeDtypeStruct((s, WIDTH), BF16),
        scratch_shapes=[pltpu.VMEM((SB_HEADS, tq, blk), F32), pltpu.VMEM((tq, hw), F32)],
        compiler_params=_cparams("parallel", "arbitrary"),
        name="sb_attn",
    )(_suffix_matrix(blk), p_mm, p_mm, p_mm, p_ew)


CONV_ROWS = 16
CONV_STAGE_ROWS = 64
CONV_SHIFT_ROWS = 40
SUBLANES = 8


def _conv_kernel(a_ref, b_ref, pa_ref, pb_ref, g_ref, w_ref, cb_ref, lng_ref, lnb_ref, o_ref, gs_ref, wb_ref, *, ts):
    i = pl.program_id(0)
    sub = SUBLANES
    halo = pa_ref[...] * _sigmoid(pb_ref[...])
    gs_ref[0, 0:CONV_HALO, :] = jnp.where(i > 0, halo, 0.0)
    for c in range(ts // CONV_STAGE_ROWS):
        rows = slice(c * CONV_STAGE_ROWS, (c + 1) * CONV_STAGE_ROWS)
        gs_ref[0, CONV_HALO + c * CONV_STAGE_ROWS:CONV_HALO + (c + 1) * CONV_STAGE_ROWS, :] = (
            a_ref[rows, :] * _sigmoid(b_ref[rows, :]))
    span = CONV_HALO + ts - sub
    for s in range(1, sub):
        for c in range(span // CONV_SHIFT_ROWS):
            r = c * CONV_SHIFT_ROWS
            gs_ref[s, r:r + CONV_SHIFT_ROWS, :] = gs_ref[0, r + s:r + s + CONV_SHIFT_ROWS, :]
    for k in range(CONV_KERNEL):
        wb_ref[k] = jnp.broadcast_to(w_ref[k:k + 1, :], (CONV_ROWS, WIDTH))
    first = CONV_HALO - (CONV_KERNEL - 1)
    bias = cb_ref[...]
    lng = lng_ref[...]
    lnb = lnb_ref[...]
    for c in range(ts // CONV_ROWS):
        base = c * CONV_ROWS
        acc = None
        for k in range(CONV_KERNEL):
            phase = (first + k) % sub
            r = base + first + k - phase
            term = gs_ref[phase, r:r + CONV_ROWS, :] * wb_ref[k]
            acc = term if acc is None else acc + term
        y = _silu(_layernorm_rows(acc + bias, lng, lnb))
        out = y * _silu(g_ref[base:base + CONV_ROWS, :])
        o_ref[base:base + CONV_ROWS, :] = out.astype(o_ref.dtype)


def _conformer_conv(p_ew, conv_w, conv_b, ln_g, ln_b, layer, *, ts=256):
    s = p_ew.shape[0]
    hb = ts // CONV_HALO
    row = lambda i: (i, 0)
    par = lambda i: (layer, 0, 0)
    return pl.pallas_call(
        functools.partial(_conv_kernel, ts=ts),
        grid=(s // ts,),
        in_specs=[
            pl.BlockSpec((ts, WIDTH), lambda i: (i, EW_BA)),
            pl.BlockSpec((ts, WIDTH), lambda i: (i, EW_BB)),
            pl.BlockSpec((CONV_HALO, WIDTH), lambda i: (jnp.maximum(i * hb - 1, 0), EW_BA)),
            pl.BlockSpec((CONV_HALO, WIDTH), lambda i: (jnp.maximum(i * hb - 1, 0), EW_BB)),
            pl.BlockSpec((ts, WIDTH), lambda i: (i, EW_BG)),
            pl.BlockSpec((None, CONV_KERNEL, WIDTH), par),
            pl.BlockSpec((None, 1, WIDTH), par),
            pl.BlockSpec((None, 1, WIDTH), par),
            pl.BlockSpec((None, 1, WIDTH), par),
        ],
        out_specs=pl.BlockSpec((ts, WIDTH), row),
        out_shape=jax.ShapeDtypeStruct((s, WIDTH), BF16),
        scratch_shapes=[pltpu.VMEM((SUBLANES, CONV_HALO + ts, WIDTH), F32),
                        pltpu.VMEM((CONV_KERNEL, CONV_ROWS, WIDTH), F32)],
        compiler_params=_cparams("parallel"),
        name="conformer_conv",
    )(p_ew, p_ew, p_ew, p_ew, p_ew, conv_w, conv_b, ln_g, ln_b)


def _sgu_kernel(u_ref, v_ref, g_ref, w_ref, bs_ref, lng_ref, lnb_ref, o_ref, *, ts):
    ck = SGU_CHUNK
    gw = WIDTH // SGU_GROUPS
    row = lax.broadcasted_iota(jnp.int32, (ck, ck), 0)
    col = lax.broadcasted_iota(jnp.int32, (ck, ck), 1)
    tril = col <= row
    lng = lng_ref[...]
    lnb = lnb_ref[...]
    ws = [jnp.where(tril, w_ref[g], 0.0).astype(BF16) for g in range(SGU_GROUPS)]
    for c in range(ts // ck):
        rows = slice(c * ck, (c + 1) * ck)
        vn = _layernorm_rows(_gelu_tanh(v_ref[rows, :]), lng, lnb).astype(BF16)
        for g in range(SGU_GROUPS):
            cols = slice(g * gw, (g + 1) * gw)
            z = jnp.dot(ws[g], vn[:, cols], preferred_element_type=F32) + bs_ref[:, cols]
            out = _gelu_tanh(u_ref[rows, cols]) * z * _silu(g_ref[rows, cols])
            o_ref[rows, cols] = out.astype(o_ref.dtype)


def _spatial_gating(p_ew, sgu_w, bs_full, ln_g, ln_b, layer, *, ts=256):
    s = p_ew.shape[0]
    par = lambda i: (layer, 0, 0)
    return pl.pallas_call(
        functools.partial(_sgu_kernel, ts=ts),
        grid=(s // ts,),
        in_specs=[
            pl.BlockSpec((ts, WIDTH), lambda i: (i, EW_CU)),
            pl.BlockSpec((ts, WIDTH), lambda i: (i, EW_CV)),
            pl.BlockSpec((ts, WIDTH), lambda i: (i, EW_CG)),
            pl.BlockSpec((None, SGU_GROUPS, SGU_CHUNK, SGU_CHUNK), lambda i: (layer, 0, 0, 0)),
            pl.BlockSpec((None, SGU_CHUNK, WIDTH), par),
            pl.BlockSpec((None, 1, WIDTH), par),
            pl.BlockSpec((None, 1, WIDTH), par),
        ],
        out_specs=pl.BlockSpec((ts, WIDTH), lambda i: (i, 0)),
        out_shape=jax.ShapeDtypeStruct((s, WIDTH), BF16),
        compiler_params=_cparams("parallel"),
        name="spatial_gating",
    )(p_ew, p_ew, p_ew, sgu_w, bs_full, ln_g, ln_b)


DIL_SUPER = 2048
DIL_COMBINE_ROWS = 256
DIL_UNROLL = 8


DIL_FREE_STRIDE = 4


def _class_reader(src_ref, start, count, dil, tmp_ref):
    if dil == 1:
        return lambda r: src_ref[start:start + count, :]
    if dil <= DIL_FREE_STRIDE:
        return lambda r: src_ref[pl.ds(start + r, count, stride=dil), :]
    f = DIL_FREE_STRIDE
    inner = dil // f
    assert inner <= f and dil % f == 0
    q = count * inner
    for a in range(f):
        tmp_ref[a * q:(a + 1) * q, :] = src_ref[pl.ds(start + a, q, stride=f), :]
    return lambda r: tmp_ref[pl.ds((r % f) * q + r // f, count, stride=inner), :]


def _classes_to_positions(src_ref, g, pos_ref, slot, dil, rows, tmp_ref):
    n = rows // dil
    if dil <= DIL_FREE_STRIDE:
        for r in range(dil):
            pos_ref[slot, pl.ds(r, n, stride=dil), :] = src_ref[g, r * n:(r + 1) * n, :]
        return
    f = DIL_FREE_STRIDE
    inner = dil // f
    q = n * inner
    for r in range(dil):
        tmp_ref[pl.ds((r % f) * q + r // f, n, stride=inner), :] = src_ref[g, r * n:(r + 1) * n, :]
    for a in range(f):
        pos_ref[slot, pl.ds(a, q, stride=f), :] = tmp_ref[a * q:(a + 1) * q, :]


def _dil_kernel(q0_ref, q1_ref, q2_ref, k0_ref, k1_ref, k2_ref, kp0_ref, kp1_ref, kp2_ref, v_ref, vp_ref, g_ref,
                o_ref, qs_ref, ks_ref, vs_ref, out_ref, lse_ref, pos_ref, tmp_ref):
    t, sup = DIL_BLOCK, DIL_SUPER
    i = pl.program_id(0)
    q_refs = (q0_ref, q1_ref, q2_ref)
    k_refs = (k0_ref, k1_ref, k2_ref)
    kp_refs = (kp0_ref, kp1_ref, kp2_ref)
    scale = HEAD_DIM ** -0.5
    row = lax.broadcasted_iota(jnp.int32, (t, 2 * t), 0)
    col = lax.broadcasted_iota(jnp.int32, (t, 2 * t), 1)
    band = jnp.logical_and(col >= row, col <= row + t)
    own = col >= t

    for g, (_, dil) in enumerate(DIL_PATTERNS):
        n = sup // dil
        sources = (
            (q_refs[g], 0, n, lambda r: (qs_ref, r * n)),
            (kp_refs[g], 0, t, lambda r: (ks_ref, r * (n + t))),
            (k_refs[g], 0, n, lambda r: (ks_ref, r * (n + t) + t)),
            (vp_ref, sup - t * dil, t, lambda r: (vs_ref, r * (n + t))),
            (v_ref, 0, n, lambda r: (vs_ref, r * (n + t) + t)),
        )
        for src_ref, start, count, dest in sources:
            read = _class_reader(src_ref, start, count, dil, tmp_ref)
            for r in range(dil):
                dst_ref, off = dest(r)
                dst_ref[g, off:off + count, :] = read(r).astype(BF16)

        per_class = n // t

        def block(u, carry, g=g, per_class=per_class):
            r = u // per_class
            bb = u - r * per_class
            qoff = pl.multiple_of(u * t, t)
            koff = pl.multiple_of((u + r) * t, t)
            q = qs_ref[g, pl.ds(qoff, t), :]
            k = ks_ref[g, pl.ds(koff, 2 * t), :]
            v = vs_ref[g, pl.ds(koff, 2 * t), :]
            s = lax.dot_general(q, k, (((1,), (1,)), ((), ())), preferred_element_type=F32) * scale
            has_prev = jnp.logical_or(i > 0, bb > 0)
            valid = jnp.logical_and(band, jnp.logical_or(own, has_prev))
            s = jnp.where(valid, s, NEG_BIG)
            m = jnp.max(s, axis=-1, keepdims=True)
            p = jnp.exp(s - m)
            den = jnp.sum(p, axis=-1, keepdims=True)
            out_ref[g, pl.ds(qoff, t), :] = jnp.dot(p.astype(BF16), v, preferred_element_type=F32) * (1.0 / den)
            lse_ref[g, pl.ds(qoff, t), :] = jnp.broadcast_to(m + jnp.log(den), (t, HEAD_DIM))
            return carry

        lax.fori_loop(0, sup // t, block, 0, unroll=DIL_UNROLL)

    slot = 0
    for g in (1, 2):
        for src_ref in (out_ref, lse_ref):
            _classes_to_positions(src_ref, g, pos_ref, slot, DIL_PATTERNS[g][1], sup, tmp_ref)
            slot += 1

    cr = DIL_COMBINE_ROWS
    for c in range(sup // cr):
        rows = slice(c * cr, (c + 1) * cr)
        l0, l1, l2 = lse_ref[0, rows, :], pos_ref[1, rows, :], pos_ref[3, rows, :]
        m = jnp.maximum(jnp.maximum(l0, l1), l2)
        e0, e1, e2 = jnp.exp(l0 - m), jnp.exp(l1 - m), jnp.exp(l2 - m)
        mix = e0 * out_ref[0, rows, :] + e1 * pos_ref[0, rows, :] + e2 * pos_ref[2, rows, :]
        o_ref[rows, :] = ((mix / (e0 + e1 + e2)) * _silu(g_ref[rows, :])).astype(o_ref.dtype)


def _dilated_mixture(p_ew):
    s = p_ew.shape[0]
    t, sup = DIL_BLOCK, DIL_SUPER
    assert all(w // d == t for w, d in DIL_PATTERNS) and sup == t * DIL_PATTERNS[-1][1]
    nh = N_HEADS
    cur = lambda blk: (lambda i, h: (i, blk * nh + h))
    specs = [pl.BlockSpec((sup, HEAD_DIM), cur(EW_DQ + g)) for g in range(3)]
    specs += [pl.BlockSpec((sup, HEAD_DIM), cur(EW_DK + g)) for g in range(3)]
    for g, (_, dil) in enumerate(DIL_PATTERNS):
        rows = t * dil
        per = sup // rows
        specs.append(pl.BlockSpec((rows, HEAD_DIM),
                                  lambda i, h, g=g, per=per: (jnp.maximum(i * per - 1, 0), (EW_DK + g) * nh + h)))
    specs.append(pl.BlockSpec((sup, HEAD_DIM), cur(EW_DV)))
    specs.append(pl.BlockSpec((sup, HEAD_DIM), lambda i, h: (jnp.maximum(i - 1, 0), EW_DV * nh + h)))
    specs.append(pl.BlockSpec((sup, HEAD_DIM), cur(EW_DG)))
    hist = max(sup + t * d for _, d in DIL_PATTERNS)
    return pl.pallas_call(
        _dil_kernel,
        grid=(s // sup, nh),
        in_specs=specs,
        out_specs=pl.BlockSpec((sup, HEAD_DIM), lambda i, h: (i, h)),
        out_shape=jax.ShapeDtypeStruct((s, WIDTH), BF16),
        scratch_shapes=[
            pltpu.VMEM((3, sup, HEAD_DIM), BF16),
            pltpu.VMEM((3, hist, HEAD_DIM), BF16),
            pltpu.VMEM((3, hist, HEAD_DIM), BF16),
            pltpu.VMEM((3, sup, HEAD_DIM), F32),
            pltpu.VMEM((3, sup, HEAD_DIM), F32),
            pltpu.VMEM((4, sup, HEAD_DIM), F32),
            pltpu.VMEM((sup, HEAD_DIM), F32),
        ],
        compiler_params=_cparams("parallel", "parallel"),
        name="dilated_attn",
    )(*([p_ew] * 12))


def _merge_kernel(x_ref, ng_ref, ya_ref, yb_ref, yc_ref, yd_ref, wg0_ref, wg1_ref, wg2_ref, wg3_ref,
                  bg_ref, wb_ref, wo_ref, o_ref, h_ref, *, tm):
    @pl.when(pl.program_id(1) == 0)
    def _():
        _norm_into(x_ref, ng_ref, h_ref, tm)
        o_ref[...] = x_ref[...]

    h = h_ref[...]
    merged = None
    branches = ((ya_ref, wg0_ref), (yb_ref, wg1_ref), (yc_ref, wg2_ref), (yd_ref, wg3_ref))
    for n, (y_ref, wg_ref) in enumerate(branches):
        gate = _sigmoid(jnp.dot(h, wg_ref[...], preferred_element_type=F32) + bg_ref[n:n + 1, :])
        term = gate * jnp.dot(y_ref[...], wb_ref[n], preferred_element_type=F32)
        merged = term if merged is None else merged + term
    o_ref[...] += jnp.dot(merged.astype(BF16), wo_ref[...], preferred_element_type=F32)


def _merge(x, norm_g, ys, w_gate, b_gate, w_branch, w_out, layer, *, tm=512, tn=256):
    s, d = x.shape
    nj = d // tn
    y_spec = pl.BlockSpec((tm, WIDTH), lambda i, j: (i, 0))
    wg_spec = lambda n: pl.BlockSpec((None, d, tn), lambda i, j: (layer, 0, n * nj + j))
    return pl.pallas_call(
        functools.partial(_merge_kernel, tm=tm),
        grid=(s // tm, nj),
        in_specs=[
            pl.BlockSpec((tm, d), lambda i, j: (i, 0)),
            pl.BlockSpec((None, 1, d), lambda i, j: (layer, 0, 0)),
            y_spec, y_spec, y_spec, y_spec,
            wg_spec(0), wg_spec(1), wg_spec(2), wg_spec(3),
            pl.BlockSpec((None, N_BRANCH, tn), lambda i, j: (layer, 0, j)),
            pl.BlockSpec((None, N_BRANCH, WIDTH, tn), lambda i, j: (layer, 0, 0, j)),
            pl.BlockSpec((None, tn, d), lambda i, j: (layer, j, 0)),
        ],
        out_specs=pl.BlockSpec((tm, d), lambda i, j: (i, 0)),
        out_shape=jax.ShapeDtypeStruct((s, d), F32),
        scratch_shapes=[pltpu.VMEM((tm, d), BF16)],
        compiler_params=_cparams("parallel", "arbitrary"),
        name="merge_out",
    )(x, norm_g, *ys, w_gate, w_gate, w_gate, w_gate, b_gate, w_branch, w_out)


def _final_norm_kernel(x_ref, g_ref, o_ref):
    o_ref[...] = _rmsnorm_rows(x_ref[...], g_ref[...])


def _final_norm(x, g, *, ts=256):
    s, d = x.shape
    return pl.pallas_call(
        _final_norm_kernel,
        grid=(s // ts,),
        in_specs=[pl.BlockSpec((ts, d), lambda i: (i, 0)), pl.BlockSpec((1, d), lambda i: (0, 0))],
        out_specs=pl.BlockSpec((ts, d), lambda i: (i, 0)),
        out_shape=jax.ShapeDtypeStruct((s, d), F32),
        compiler_params=_cparams("parallel"),
        name="final_norm",
    )(x, g)


def kernel(x, norm_g, w_in, conv_w, conv_b, conv_ln_g, conv_ln_b, sgu_ln_g, sgu_ln_b, sgu_w, sgu_b, w_branch, w_gate, b_gate, w_out, final_g):
    batch, s, d = x.shape
    assert batch == 1 and d == D_MODEL
    depth = norm_g.shape[0]
    w_in_b = w_in.astype(BF16)
    w_gate_b = w_gate.astype(BF16)
    w_branch_b = w_branch.astype(BF16)
    w_out_b = w_out.astype(BF16)
    as_rows = lambda a: a.reshape(depth, 1, a.shape[-1])
    norm_g3, conv_b3 = as_rows(norm_g), as_rows(conv_b)
    conv_ln_g3, conv_ln_b3 = as_rows(conv_ln_g), as_rows(conv_ln_b)
    sgu_ln_g3, sgu_ln_b3 = as_rows(sgu_ln_g), as_rows(sgu_ln_b)
    b_gate3 = b_gate.reshape(depth, N_BRANCH, d)
    bs_full = jnp.repeat(jnp.swapaxes(sgu_b, 1, 2), WIDTH // SGU_GROUPS, axis=2)

    xs = x.reshape(s, d)
    for l in range(depth):
        p_mm, p_ew = _project(xs, norm_g3, w_in_b, l)
        ya = _stick_breaking(p_mm, p_ew)
        yb = _conformer_conv(p_ew, conv_w, conv_b3, conv_ln_g3, conv_ln_b3, l)
        yc = _spatial_gating(p_ew, sgu_w, bs_full, sgu_ln_g3, sgu_ln_b3, l)
        yd = _dilated_mixture(p_ew)
        xs = _merge(xs, norm_g3, (ya, yb, yc, yd), w_gate_b, b_gate3, w_branch_b, w_out_b, l)
    return _final_norm(xs, final_g.reshape(1, d)).reshape(batch, s, d)
```

```python
import functools

import numpy as np
import jax
import jax.numpy as jnp
from jax import lax
from jax.experimental import pallas as pl
from jax.experimental.pallas import tpu as pltpu

F32 = jnp.float32
BF16 = jnp.bfloat16

D_MODEL = 2048
HEAD_DIM = 128
NORM_EPS = 1e-6
N_HEADS = 8
WIDTH = N_HEADS * HEAD_DIM
CONV_KERNEL = 31
CONV_HALO = 32
SGU_GROUPS = 8
SGU_CHUNK = 128
DIL_PATTERNS = ((128, 1), (512, 4), (2048, 16))
DIL_BLOCK = 128
N_BRANCH = 4
IN_WIDTH = 18432

MM_WIDTH = 3072
EW_WIDTH = IN_WIDTH - MM_WIDTH
MM_Q, MM_K, MM_V = 0, 1, 2
EW_AG, EW_BA, EW_BB, EW_BG, EW_CU, EW_CV, EW_CG, EW_DQ, EW_DK, EW_DV, EW_DG = 0, 1, 2, 3, 4, 5, 6, 7, 10, 13, 14

SB_DEAD = -104.0
NEG_BIG = -1e30
VMEM_LIMIT = 56 * 1024 * 1024


def _cparams(*sem):
    return pltpu.CompilerParams(dimension_semantics=sem, vmem_limit_bytes=VMEM_LIMIT)


def _sigmoid(x):
    return 1.0 / (1.0 + jnp.exp(-x))


def _silu(x):
    return x * _sigmoid(x)


def _gelu_tanh(x):
    c = np.sqrt(2.0 / np.pi).astype(np.float32)
    return x * (0.5 * (1.0 + jnp.tanh(c * (x + 0.044715 * (x * x * x)))))


def _rmsnorm_rows(x, g):
    ms = jnp.mean(x * x, axis=-1, keepdims=True)
    return x * lax.rsqrt(ms + NORM_EPS) * g


def _layernorm_rows(x, g, b):
    mu = jnp.mean(x, axis=-1, keepdims=True)
    xc = x - mu
    var = jnp.mean(xc * xc, axis=-1, keepdims=True)
    return xc * lax.rsqrt(var + NORM_EPS) * g + b


NORM_ROWS = 256


def _norm_into(x_ref, g_ref, h_ref, rows):
    g = g_ref[...]
    for c in range(rows // NORM_ROWS):
        sl = slice(c * NORM_ROWS, (c + 1) * NORM_ROWS)
        h_ref[sl, :] = _rmsnorm_rows(x_ref[sl, :], g).astype(h_ref.dtype)


def _first_norm_kernel(x_ref, g_ref, h_ref, *, ts):
    _norm_into(x_ref, g_ref, h_ref, ts)


def _first_norm(x, norm_g, *, ts=512):
    s, d = x.shape
    return pl.pallas_call(
        functools.partial(_first_norm_kernel, ts=ts),
        grid=(s // ts,),
        in_specs=[pl.BlockSpec((ts, d), lambda i: (i, 0)), pl.BlockSpec((None, 1, d), lambda i: (0, 0, 0))],
        out_specs=pl.BlockSpec((ts, d), lambda i: (i, 0)),
        out_shape=jax.ShapeDtypeStruct((s, d), BF16),
        compiler_params=_cparams("parallel"),
        name="first_norm",
    )(x, norm_g)


CAST_ROWS = 256


def _proj_kernel(h_ref, w_ref, o_ref, wb_ref):
    @pl.when(pl.program_id(1) == 0)
    def _():
        for c in range(w_ref.shape[0] // CAST_ROWS):
            sl = slice(c * CAST_ROWS, (c + 1) * CAST_ROWS)
            wb_ref[sl, :] = w_ref[sl, :].astype(BF16)

    o_ref[...] = jnp.dot(h_ref[...], wb_ref[...], preferred_element_type=F32).astype(o_ref.dtype)


def _project_cols(h, w_in, layer, col0, width, out_dtype, *, tm=1024, tn=1024):
    s, d = h.shape
    j0 = col0 // tn
    return pl.pallas_call(
        _proj_kernel,
        grid=(width // tn, s // tm),
        in_specs=[
            pl.BlockSpec((tm, d), lambda j, i: (i, 0)),
            pl.BlockSpec((None, d, tn), lambda j, i: (layer, 0, j0 + j)),
        ],
        out_specs=pl.BlockSpec((tm, tn), lambda j, i: (i, j)),
        out_shape=jax.ShapeDtypeStruct((s, width), out_dtype),
        scratch_shapes=[pltpu.VMEM((d, tn), BF16)],
        compiler_params=_cparams("arbitrary", "arbitrary"),
        name="in_proj",
    )(h, w_in)


def _project(h, w_in, layer):
    p_mm = _project_cols(h, w_in, layer, 0, MM_WIDTH, BF16)
    p_ew = _project_cols(h, w_in, layer, MM_WIDTH, EW_WIDTH, F32)
    return p_mm, p_ew


SB_BLOCK = 128
SB_ROWS = 256
SB_HEADS = 4


def _sb_kernel(uu_ref, q_ref, k_ref, v_ref, g_ref, o_ref, carry_ref, acc_ref):
    blk, tq = SB_BLOCK, SB_ROWS
    nsub = tq // blk
    i = pl.program_id(1)
    uu = uu_ref[...]
    scale = HEAD_DIM ** -0.5

    def step(j, r0, masked):
        start = pl.multiple_of(j * blk, blk)
        rows = slice(r0, tq)
        if masked:
            row = lax.broadcasted_iota(jnp.int32, (tq - r0, blk), 0)
            col = lax.broadcasted_iota(jnp.int32, (tq - r0, blk), 1)
            mask = col < row
        heads = range(SB_HEADS)
        cols = [slice(h * HEAD_DIM, (h + 1) * HEAD_DIM) for h in heads]
        carries = [carry_ref[h, rows, :] for h in heads]
        accs = [acc_ref[rows, cols[h]] for h in heads]
        nt = (((1,), (1,)), ((), ()))
        zs = [lax.dot_general(q_ref[rows, cols[h]], k_ref[pl.ds(start, blk), cols[h]], nt,
                              preferred_element_type=F32) * scale for h in heads]
        lbetas, splits = [], []
        for h in heads:
            sp = jnp.maximum(zs[h], 0.0) + jnp.log(1.0 + jnp.exp(-jnp.abs(zs[h])))
            l1mb = -sp
            if masked:
                l1mb = jnp.where(mask, l1mb, 0.0)
            hi = l1mb.astype(BF16)
            lo = (l1mb - hi.astype(F32)).astype(BF16)
            lbetas.append(zs[h] - sp)
            splits.append(jnp.concatenate([hi, lo], axis=1))
        rs = [jnp.dot(splits[h], uu, preferred_element_type=F32) for h in heads]
        ps = []
        for h in heads:
            p = jnp.exp(lbetas[h] + (carries[h] + rs[h][:, :blk]))
            if masked:
                p = jnp.where(mask, p, 0.0)
            ps.append(p.astype(BF16))
            carries[h] = carries[h] + rs[h][:, blk:]
        for h in heads:
            accs[h] = accs[h] + jnp.dot(ps[h], v_ref[pl.ds(start, blk), cols[h]], preferred_element_type=F32)
        for h in heads:
            carry_ref[h, rows, :] = carries[h]
            acc_ref[rows, cols[h]] = accs[h]
        top = functools.reduce(jnp.maximum, carries)
        return jnp.max(top)

    carry_ref[...] = jnp.zeros_like(carry_ref)
    acc_ref[...] = jnp.zeros_like(acc_ref)
    for c in reversed(range(nsub)):
        alive = step(i * nsub + c, c * blk, True)

    def cond(state):
        j, top = state
        return jnp.logical_and(j >= 0, top > SB_DEAD)

    def body(state):
        j, _ = state
        return j - 1, step(j, 0, False)

    lax.while_loop(cond, body, (i * nsub - 1, alive))
    o_ref[...] = (acc_ref[...] * _silu(g_ref[...])).astype(o_ref.dtype)


def _suffix_matrix(blk):
    j = np.arange(blk)[:, None]
    s = np.arange(blk)[None, :]
    half = np.concatenate([(j > s).astype(np.float32), np.ones((blk, blk), np.float32)], axis=1)
    return jnp.asarray(np.concatenate([half, half], axis=0), dtype=BF16)


def _stick_breaking(p_mm, p_ew):
    s = p_mm.shape[0]
    blk, tq, hw = SB_BLOCK, SB_ROWS, SB_HEADS * HEAD_DIM
    hb = WIDTH // hw
    return pl.pallas_call(
        _sb_kernel,
        grid=(hb, s // tq),
        in_specs=[
            pl.BlockSpec((2 * blk, 2 * blk), lambda h, i: (0, 0)),
            pl.BlockSpec((tq, hw), lambda h, i: (i, MM_Q * hb + h)),
            pl.BlockSpec((s, hw), lambda h, i: (0, MM_K * hb + h)),
            pl.BlockSpec((s, hw), lambda h, i: (0, MM_V * hb + h)),
            pl.BlockSpec((tq, hw), lambda h, i: (i, EW_AG * hb + h)),
        ],
        out_specs=pl.BlockSpec((tq, hw), lambda h, i: (i, h)),
        out_shape=jax.ShapeDtypeStruct((s, WIDTH), BF16),
        scratch_shapes=[pltpu.VMEM((SB_HEADS, tq, blk), F32), pltpu.VMEM((tq, hw), F32)],
        compiler_params=_cparams("parallel", "arbitrary"),
        name="sb_attn",
    )(_suffix_matrix(blk), p_mm, p_mm, p_mm, p_ew)


CONV_ROWS = 16
CONV_STAGE_ROWS = 64
CONV_SHIFT_ROWS = 40
SUBLANES = 8


def _conv_kernel(a_ref, b_ref, pa_ref, pb_ref, g_ref, w_ref, cb_ref, lng_ref, lnb_ref, o_ref, gs_ref, wb_ref, *, ts):
    i = pl.program_id(0)
    sub = SUBLANES
    halo = pa_ref[...] * _sigmoid(pb_ref[...])
    gs_ref[0, 0:CONV_HALO, :] = jnp.where(i > 0, halo, 0.0)
    for c in range(ts // CONV_STAGE_ROWS):
        rows = slice(c * CONV_STAGE_ROWS, (c + 1) * CONV_STAGE_ROWS)
        gs_ref[0, CONV_HALO + c * CONV_STAGE_ROWS:CONV_HALO + (c + 1) * CONV_STAGE_ROWS, :] = (
            a_ref[rows, :] * _sigmoid(b_ref[rows, :]))
    span = CONV_HALO + ts - sub
    for s in range(1, sub):
        for c in range(span // CONV_SHIFT_ROWS):
            r = c * CONV_SHIFT_ROWS
            gs_ref[s, r:r + CONV_SHIFT_ROWS, :] = gs_ref[0, r + s:r + s + CONV_SHIFT_ROWS, :]
    for k in range(CONV_KERNEL):
        wb_ref[k] = jnp.broadcast_to(w_ref[k:k + 1, :], (CONV_ROWS, WIDTH))
    first = CONV_HALO - (CONV_KERNEL - 1)
    bias = cb_ref[...]
    lng = lng_ref[...]
    lnb = lnb_ref[...]
    for c in range(ts // CONV_ROWS):
        base = c * CONV_ROWS
        acc = None
        for k in range(CONV_KERNEL):
            phase = (first + k) % sub
            r = base + first + k - phase
            term = gs_ref[phase, r:r + CONV_ROWS, :] * wb_ref[k]
            acc = term if acc is None else acc + term
        y = _silu(_layernorm_rows(acc + bias, lng, lnb))
        out = y * _silu(g_ref[base:base + CONV_ROWS, :])
        o_ref[base:base + CONV_ROWS, :] = out.astype(o_ref.dtype)


def _conformer_conv(p_ew, conv_w, conv_b, ln_g, ln_b, layer, *, ts=256):
    s = p_ew.shape[0]
    hb = ts // CONV_HALO
    row = lambda i: (i, 0)
    par = lambda i: (layer, 0, 0)
    return pl.pallas_call(
        functools.partial(_conv_kernel, ts=ts),
        grid=(s // ts,),
        in_specs=[
            pl.BlockSpec((ts, WIDTH), lambda i: (i, EW_BA)),
            pl.BlockSpec((ts, WIDTH), lambda i: (i, EW_BB)),
            pl.BlockSpec((CONV_HALO, WIDTH), lambda i: (jnp.maximum(i * hb - 1, 0), EW_BA)),
            pl.BlockSpec((CONV_HALO, WIDTH), lambda i: (jnp.maximum(i * hb - 1, 0), EW_BB)),
            pl.BlockSpec((ts, WIDTH), lambda i: (i, EW_BG)),
            pl.BlockSpec((None, CONV_KERNEL, WIDTH), par),
            pl.BlockSpec((None, 1, WIDTH), par),
            pl.BlockSpec((None, 1, WIDTH), par),
            pl.BlockSpec((None, 1, WIDTH), par),
        ],
        out_specs=pl.BlockSpec((ts, WIDTH), row),
        out_shape=jax.ShapeDtypeStruct((s, WIDTH), BF16),
        scratch_shapes=[pltpu.VMEM((SUBLANES, CONV_HALO + ts, WIDTH), F32),
                        pltpu.VMEM((CONV_KERNEL, CONV_ROWS, WIDTH), F32)],
        compiler_params=_cparams("parallel"),
        name="conformer_conv",
    )(p_ew, p_ew, p_ew, p_ew, p_ew, conv_w, conv_b, ln_g, ln_b)


def _sgu_kernel(u_ref, v_ref, g_ref, w_ref, bs_ref, lng_ref, lnb_ref, o_ref, *, ts):
    ck = SGU_CHUNK
    gw = WIDTH // SGU_GROUPS
    row = lax.broadcasted_iota(jnp.int32, (ck, ck), 0)
    col = lax.broadcasted_iota(jnp.int32, (ck, ck), 1)
    tril = col <= row
    lng = lng_ref[...]
    lnb = lnb_ref[...]
    ws = [jnp.where(tril, w_ref[g], 0.0).astype(BF16) for g in range(SGU_GROUPS)]
    for c in range(ts // ck):
        rows = slice(c * ck, (c + 1) * ck)
        vn = _layernorm_rows(_gelu_tanh(v_ref[rows, :]), lng, lnb).astype(BF16)
        for g in range(SGU_GROUPS):
            cols = slice(g * gw, (g + 1) * gw)
            z = jnp.dot(ws[g], vn[:, cols], preferred_element_type=F32) + bs_ref[:, cols]
            out = _gelu_tanh(u_ref[rows, cols]) * z * _silu(g_ref[rows, cols])
            o_ref[rows, cols] = out.astype(o_ref.dtype)


def _spatial_gating(p_ew, sgu_w, bs_full, ln_g, ln_b, layer, *, ts=256):
    s = p_ew.shape[0]
    par = lambda i: (layer, 0, 0)
    return pl.pallas_call(
        functools.partial(_sgu_kernel, ts=ts),
        grid=(s // ts,),
        in_specs=[
            pl.BlockSpec((ts, WIDTH), lambda i: (i, EW_CU)),
            pl.BlockSpec((ts, WIDTH), lambda i: (i, EW_CV)),
            pl.BlockSpec((ts, WIDTH), lambda i: (i, EW_CG)),
            pl.BlockSpec((None, SGU_GROUPS, SGU_CHUNK, SGU_CHUNK), lambda i: (layer, 0, 0, 0)),
            pl.BlockSpec((None, SGU_CHUNK, WIDTH), par),
            pl.BlockSpec((None, 1, WIDTH), par),
            pl.BlockSpec((None, 1, WIDTH), par),
        ],
        out_specs=pl.BlockSpec((ts, WIDTH), lambda i: (i, 0)),
        out_shape=jax.ShapeDtypeStruct((s, WIDTH), BF16),
        compiler_params=_cparams("parallel"),
        name="spatial_gating",
    )(p_ew, p_ew, p_ew, sgu_w, bs_full, ln_g, ln_b)


DIL_SUPER = 2048
DIL_COMBINE_ROWS = 256
DIL_UNROLL = 8


DIL_FREE_STRIDE = 4


def _class_reader(src_ref, start, count, dil, tmp_ref):
    if dil == 1:
        return lambda r: src_ref[start:start + count, :]
    if dil <= DIL_FREE_STRIDE:
        return lambda r: src_ref[pl.ds(start + r, count, stride=dil), :]
    f = DIL_FREE_STRIDE
    inner = dil // f
    assert inner <= f and dil % f == 0
    q = count * inner
    for a in range(f):
        tmp_ref[a * q:(a + 1) * q, :] = src_ref[pl.ds(start + a, q, stride=f), :]
    return lambda r: tmp_ref[pl.ds((r % f) * q + r // f, count, stride=inner), :]


def _classes_to_positions(src_ref, g, pos_ref, slot, dil, rows, tmp_ref):
    n = rows // dil
    if dil <= DIL_FREE_STRIDE:
        for r in range(dil):
            pos_ref[slot, pl.ds(r, n, stride=dil), :] = src_ref[g, r * n:(r + 1) * n, :]
        return
    f = DIL_FREE_STRIDE
    inner = dil // f
    q = n * inner
    for r in range(dil):
        tmp_ref[pl.ds((r % f) * q + r // f, n, stride=inner), :] = src_ref[g, r * n:(r + 1) * n, :]
    for a in range(f):
        pos_ref[slot, pl.ds(a, q, stride=f), :] = tmp_ref[a * q:(a + 1) * q, :]


def _dil_kernel(q0_ref, q1_ref, q2_ref, k0_ref, k1_ref, k2_ref, kp0_ref, kp1_ref, kp2_ref, v_ref, vp_ref, g_ref,
                o_ref, qs_ref, ks_ref, vs_ref, out_ref, lse_ref, pos_ref, tmp_ref):
    t, sup = DIL_BLOCK, DIL_SUPER
    i = pl.program_id(0)
    q_refs = (q0_ref, q1_ref, q2_ref)
    k_refs = (k0_ref, k1_ref, k2_ref)
    kp_refs = (kp0_ref, kp1_ref, kp2_ref)
    scale = HEAD_DIM ** -0.5
    row = lax.broadcasted_iota(jnp.int32, (t, 2 * t), 0)
    col = lax.broadcasted_iota(jnp.int32, (t, 2 * t), 1)
    band = jnp.logical_and(col >= row, col <= row + t)
    own = col >= t

    for g, (_, dil) in enumerate(DIL_PATTERNS):
        n = sup // dil
        sources = (
            (q_refs[g], 0, n, lambda r: (qs_ref, r * n)),
            (kp_refs[g], 0, t, lambda r: (ks_ref, r * (n + t))),
            (k_refs[g], 0, n, lambda r: (ks_ref, r * (n + t) + t)),
            (vp_ref, sup - t * dil, t, lambda r: (vs_ref, r * (n + t))),
            (v_ref, 0, n, lambda r: (vs_ref, r * (n + t) + t)),
        )
        for src_ref, start, count, dest in sources:
            read = _class_reader(src_ref, start, count, dil, tmp_ref)
            for r in range(dil):
                dst_ref, off = dest(r)
                dst_ref[g, off:off + count, :] = read(r).astype(BF16)

        per_class = n // t

        def block(u, carry, g=g, per_class=per_class):
            r = u // per_class
            bb = u - r * per_class
            qoff = pl.multiple_of(u * t, t)
            koff = pl.multiple_of((u + r) * t, t)
            q = qs_ref[g, pl.ds(qoff, t), :]
            k = ks_ref[g, pl.ds(koff, 2 * t), :]
            v = vs_ref[g, pl.ds(koff, 2 * t), :]
            s = lax.dot_general(q, k, (((1,), (1,)), ((), ())), preferred_element_type=F32) * scale
            has_prev = jnp.logical_or(i > 0, bb > 0)
            valid = jnp.logical_and(band, jnp.logical_or(own, has_prev))
            s = jnp.where(valid, s, NEG_BIG)
            m = jnp.max(s, axis=-1, keepdims=True)
            p = jnp.exp(s - m)
            den = jnp.sum(p, axis=-1, keepdims=True)
            out_ref[g, pl.ds(qoff, t), :] = jnp.dot(p.astype(BF16), v, preferred_element_type=F32) * (1.0 / den)
            lse_ref[g, pl.ds(qoff, t), :] = jnp.broadcast_to(m + jnp.log(den), (t, HEAD_DIM))
            return carry

        lax.fori_loop(0, sup // t, block, 0, unroll=DIL_UNROLL)

    slot = 0
    for g in (1, 2):
        for src_ref in (out_ref, lse_ref):
            _classes_to_positions(src_ref, g, pos_ref, slot, DIL_PATTERNS[g][1], sup, tmp_ref)
            slot += 1

    cr = DIL_COMBINE_ROWS
    for c in range(sup // cr):
        rows = slice(c * cr, (c + 1) * cr)
        l0, l1, l2 = lse_ref[0, rows, :], pos_ref[1, rows, :], pos_ref[3, rows, :]
        m = jnp.maximum(jnp.maximum(l0, l1), l2)
        e0, e1, e2 = jnp.exp(l0 - m), jnp.exp(l1 - m), jnp.exp(l2 - m)
        mix = e0 * out_ref[0, rows, :] + e1 * pos_ref[0, rows, :] + e2 * pos_ref[2, rows, :]
        o_ref[rows, :] = ((mix / (e0 + e1 + e2)) * _silu(g_ref[rows, :])).astype(o_ref.dtype)


def _dilated_mixture(p_ew):
    s = p_ew.shape[0]
    t, sup = DIL_BLOCK, DIL_SUPER
    assert all(w // d == t for w, d in DIL_PATTERNS) and sup == t * DIL_PATTERNS[-1][1]
    nh = N_HEADS
    cur = lambda blk: (lambda i, h: (i, blk * nh + h))
    specs = [pl.BlockSpec((sup, HEAD_DIM), cur(EW_DQ + g)) for g in range(3)]
    specs += [pl.BlockSpec((sup, HEAD_DIM), cur(EW_DK + g)) for g in range(3)]
    for g, (_, dil) in enumerate(DIL_PATTERNS):
        rows = t * dil
        per = sup // rows
        specs.append(pl.BlockSpec((rows, HEAD_DIM),
                                  lambda i, h, g=g, per=per: (jnp.maximum(i * per - 1, 0), (EW_DK + g) * nh + h)))
    specs.append(pl.BlockSpec((sup, HEAD_DIM), cur(EW_DV)))
    specs.append(pl.BlockSpec((sup, HEAD_DIM), lambda i, h: (jnp.maximum(i - 1, 0), EW_DV * nh + h)))
    specs.append(pl.BlockSpec((sup, HEAD_DIM), cur(EW_DG)))
    hist = max(sup + t * d for _, d in DIL_PATTERNS)
    return pl.pallas_call(
        _dil_kernel,
        grid=(s // sup, nh),
        in_specs=specs,
        out_specs=pl.BlockSpec((sup, HEAD_DIM), lambda i, h: (i, h)),
        out_shape=jax.ShapeDtypeStruct((s, WIDTH), BF16),
        scratch_shapes=[
            pltpu.VMEM((3, sup, HEAD_DIM), BF16),
            pltpu.VMEM((3, hist, HEAD_DIM), BF16),
            pltpu.VMEM((3, hist, HEAD_DIM), BF16),
            pltpu.VMEM((3, sup, HEAD_DIM), F32),
            pltpu.VMEM((3, sup, HEAD_DIM), F32),
            pltpu.VMEM((4, sup, HEAD_DIM), F32),
            pltpu.VMEM((sup, HEAD_DIM), F32),
        ],
        compiler_params=_cparams("parallel", "parallel"),
        name="dilated_attn",
    )(*([p_ew] * 12))


def _merge_kernel(x_ref, h_ref, gn_ref, ya_ref, yb_ref, yc_ref, yd_ref, wg0_ref, wg1_ref, wg2_ref, wg3_ref,
                  bg_ref, wb_ref, wo_ref, o_ref, hn_ref, *, tm):
    j = pl.program_id(1)

    @pl.when(j == 0)
    def _():
        o_ref[...] = x_ref[...]

    h = h_ref[...]
    merged = None
    branches = ((ya_ref, wg0_ref), (yb_ref, wg1_ref), (yc_ref, wg2_ref), (yd_ref, wg3_ref))
    for n, (y_ref, wg_ref) in enumerate(branches):
        gate = _sigmoid(jnp.dot(h, wg_ref[...], preferred_element_type=F32) + bg_ref[n:n + 1, :])
        term = gate * jnp.dot(y_ref[...], wb_ref[n], preferred_element_type=F32)
        merged = term if merged is None else merged + term
    o_ref[...] += jnp.dot(merged.astype(BF16), wo_ref[...], preferred_element_type=F32)

    @pl.when(j == pl.num_programs(1) - 1)
    def _():
        _norm_into(o_ref, gn_ref, hn_ref, tm)


def _merge(x, h, next_g, ys, w_gate, b_gate, w_branch, w_out, layer, norm_dtype, *, tm=512, tn=256):
    s, d = x.shape
    nj = d // tn
    y_spec = pl.BlockSpec((tm, WIDTH), lambda i, j: (i, 0))
    row_spec = pl.BlockSpec((tm, d), lambda i, j: (i, 0))
    wg_spec = lambda n: pl.BlockSpec((None, d, tn), lambda i, j: (layer, 0, n * nj + j))
    return pl.pallas_call(
        functools.partial(_merge_kernel, tm=tm),
        grid=(s // tm, nj),
        in_specs=[
            row_spec, row_spec,
            pl.BlockSpec((None, 1, d), lambda i, j: (layer, 0, 0)),
            y_spec, y_spec, y_spec, y_spec,
            wg_spec(0), wg_spec(1), wg_spec(2), wg_spec(3),
            pl.BlockSpec((None, N_BRANCH, tn), lambda i, j: (layer, 0, j)),
            pl.BlockSpec((None, N_BRANCH, WIDTH, tn), lambda i, j: (layer, 0, 0, j)),
            pl.BlockSpec((None, tn, d), lambda i, j: (layer, j, 0)),
        ],
        out_specs=[row_spec, row_spec],
        out_shape=[jax.ShapeDtypeStruct((s, d), F32), jax.ShapeDtypeStruct((s, d), norm_dtype)],
        compiler_params=_cparams("parallel", "arbitrary"),
        name="merge_out",
    )(x, h, next_g, *ys, w_gate, w_gate, w_gate, w_gate, b_gate, w_branch, w_out)


def kernel(x, norm_g, w_in, conv_w, conv_b, conv_ln_g, conv_ln_b, sgu_ln_g, sgu_ln_b, sgu_w, sgu_b, w_branch, w_gate, b_gate, w_out, final_g):
    batch, s, d = x.shape
    assert batch == 1 and d == D_MODEL
    depth = norm_g.shape[0]
    w_gate_b = w_gate.astype(BF16)
    w_branch_b = w_branch.astype(BF16)
    w_out_b = w_out.astype(BF16)
    as_rows = lambda a: a.reshape(depth, 1, a.shape[-1])
    norm_g3, conv_b3 = as_rows(norm_g), as_rows(conv_b)
    conv_ln_g3, conv_ln_b3 = as_rows(conv_ln_g), as_rows(conv_ln_b)
    sgu_ln_g3, sgu_ln_b3 = as_rows(sgu_ln_g), as_rows(sgu_ln_b)
    b_gate3 = b_gate.reshape(depth, N_BRANCH, d)
    bs_full = jnp.repeat(jnp.swapaxes(sgu_b, 1, 2), WIDTH // SGU_GROUPS, axis=2)

    next_g3 = jnp.concatenate([norm_g3[1:], final_g.reshape(1, 1, d)], axis=0)

    xs = x.reshape(s, d)
    h = _first_norm(xs, norm_g3)
    for l in range(depth):
        p_mm, p_ew = _project(h, w_in, l)
        ya = _stick_breaking(p_mm, p_ew)
        yb = _conformer_conv(p_ew, conv_w, conv_b3, conv_ln_g3, conv_ln_b3, l)
        yc = _spatial_gating(p_ew, sgu_w, bs_full, sgu_ln_g3, sgu_ln_b3, l)
        yd = _dilated_mixture(p_ew)
        norm_dtype = F32 if l == depth - 1 else BF16
        xs, h = _merge(xs, h, next_g3, (ya, yb, yc, yd), w_gate_b, b_gate3, w_branch_b, w_out_b, l, norm_dtype)
    return h.reshape(batch, s, d)
```

```python
import functools

import numpy as np
import jax
import jax.numpy as jnp
from jax import lax
from jax.experimental import pallas as pl
from jax.experimental.pallas import tpu as pltpu

F32 = jnp.float32
BF16 = jnp.bfloat16

D_MODEL = 2048
HEAD_DIM = 128
NORM_EPS = 1e-6
N_HEADS = 8
WIDTH = N_HEADS * HEAD_DIM
CONV_KERNEL = 31
CONV_HALO = 32
SGU_GROUPS = 8
SGU_CHUNK = 128
DIL_PATTERNS = ((128, 1), (512, 4), (2048, 16))
DIL_BLOCK = 128
N_BRANCH = 4
IN_WIDTH = 18432

MM_WIDTH = 3072
EW_WIDTH = IN_WIDTH - MM_WIDTH
MM_Q, MM_K, MM_V = 0, 1, 2
EW_AG, EW_BA, EW_BB, EW_BG, EW_CU, EW_CV, EW_CG, EW_DQ, EW_DK, EW_DV, EW_DG = 0, 1, 2, 3, 4, 5, 6, 7, 10, 13, 14

SB_DEAD = -104.0
NEG_BIG = -1e30
VMEM_LIMIT = 56 * 1024 * 1024


def _cparams(*sem):
    return pltpu.CompilerParams(dimension_semantics=sem, vmem_limit_bytes=VMEM_LIMIT)


def _sigmoid(x):
    return 1.0 / (1.0 + jnp.exp(-x))


def _silu(x):
    return x * _sigmoid(x)


def _gelu_tanh(x):
    c = np.sqrt(2.0 / np.pi).astype(np.float32)
    return x * (0.5 * (1.0 + jnp.tanh(c * (x + 0.044715 * (x * x * x)))))


def _rmsnorm_rows(x, g):
    ms = jnp.mean(x * x, axis=-1, keepdims=True)
    return x * lax.rsqrt(ms + NORM_EPS) * g


def _layernorm_rows(x, g, b):
    mu = jnp.mean(x, axis=-1, keepdims=True)
    xc = x - mu
    var = jnp.mean(xc * xc, axis=-1, keepdims=True)
    return xc * lax.rsqrt(var + NORM_EPS) * g + b


NORM_ROWS = 256


def _norm_into(x_ref, g_ref, h_ref, rows):
    g = g_ref[...]
    for c in range(rows // NORM_ROWS):
        sl = slice(c * NORM_ROWS, (c + 1) * NORM_ROWS)
        h_ref[sl, :] = _rmsnorm_rows(x_ref[sl, :], g).astype(h_ref.dtype)


def _first_norm_kernel(x_ref, g_ref, h_ref, *, ts):
    _norm_into(x_ref, g_ref, h_ref, ts)


def _first_norm(x, norm_g, *, ts=512):
    s, d = x.shape
    return pl.pallas_call(
        functools.partial(_first_norm_kernel, ts=ts),
        grid=(s // ts,),
        in_specs=[pl.BlockSpec((ts, d), lambda i: (i, 0)), pl.BlockSpec((None, 1, d), lambda i: (0, 0, 0))],
        out_specs=pl.BlockSpec((ts, d), lambda i: (i, 0)),
        out_shape=jax.ShapeDtypeStruct((s, d), BF16),
        compiler_params=_cparams("parallel"),
        name="first_norm",
    )(x, norm_g)


CAST_ROWS = 256


def _proj_kernel(h_ref, w_ref, o_ref, wb_ref):
    @pl.when(pl.program_id(1) == 0)
    def _():
        for c in range(w_ref.shape[0] // CAST_ROWS):
            sl = slice(c * CAST_ROWS, (c + 1) * CAST_ROWS)
            wb_ref[sl, :] = w_ref[sl, :].astype(BF16)

    o_ref[...] = jnp.dot(h_ref[...], wb_ref[...], preferred_element_type=F32).astype(o_ref.dtype)


def _project_cols(h, w_in, layer, col0, width, out_dtype, *, tm=1024, tn=1024):
    s, d = h.shape
    j0 = col0 // tn
    return pl.pallas_call(
        _proj_kernel,
        grid=(width // tn, s // tm),
        in_specs=[
            pl.BlockSpec((tm, d), lambda j, i: (i, 0)),
            pl.BlockSpec((None, d, tn), lambda j, i: (layer, 0, j0 + j)),
        ],
        out_specs=pl.BlockSpec((tm, tn), lambda j, i: (i, j)),
        out_shape=jax.ShapeDtypeStruct((s, width), out_dtype),
        scratch_shapes=[pltpu.VMEM((d, tn), BF16)],
        compiler_params=_cparams("arbitrary", "arbitrary"),
        name="in_proj",
    )(h, w_in)


def _project(h, w_in, layer):
    p_mm = _project_cols(h, w_in, layer, 0, MM_WIDTH, BF16)
    p_ew = _project_cols(h, w_in, layer, MM_WIDTH, EW_WIDTH, F32)
    return p_mm, p_ew


SB_BLOCK = 128
SB_ROWS = 256
SB_HEADS = 4


def _sb_kernel(uu_ref, q_ref, k_ref, v_ref, g_ref, o_ref, carry_ref, acc_ref):
    blk, tq = SB_BLOCK, SB_ROWS
    nsub = tq // blk
    i = pl.program_id(1)
    uu = uu_ref[...]
    scale = HEAD_DIM ** -0.5

    def step(j, r0, masked):
        start = pl.multiple_of(j * blk, blk)
        rows = slice(r0, tq)
        if masked:
            row = lax.broadcasted_iota(jnp.int32, (tq - r0, blk), 0)
            col = lax.broadcasted_iota(jnp.int32, (tq - r0, blk), 1)
            mask = col < row
        heads = range(SB_HEADS)
        cols = [slice(h * HEAD_DIM, (h + 1) * HEAD_DIM) for h in heads]
        carries = [carry_ref[h, rows, :] for h in heads]
        accs = [acc_ref[rows, cols[h]] for h in heads]
        nt = (((1,), (1,)), ((), ()))
        zs = [lax.dot_general(q_ref[rows, cols[h]], k_ref[pl.ds(start, blk), cols[h]], nt,
                              preferred_element_type=F32) * scale for h in heads]
        lbetas, splits = [], []
        for h in heads:
            sp = jnp.maximum(zs[h], 0.0) + jnp.log(1.0 + jnp.exp(-jnp.abs(zs[h])))
            l1mb = -sp
            if masked:
                l1mb = jnp.where(mask, l1mb, 0.0)
            hi = l1mb.astype(BF16)
            lo = (l1mb - hi.astype(F32)).astype(BF16)
            lbetas.append(zs[h] - sp)
            splits.append(jnp.concatenate([hi, lo], axis=1))
        rs = [jnp.dot(splits[h], uu, preferred_element_type=F32) for h in heads]
        ps = []
        for h in heads:
            p = jnp.exp(lbetas[h] + (carries[h] + rs[h][:, :blk]))
            if masked:
                p = jnp.where(mask, p, 0.0)
            ps.append(p.astype(BF16))
            carries[h] = carries[h] + rs[h][:, blk:]
        for h in heads:
            accs[h] = accs[h] + jnp.dot(ps[h], v_ref[pl.ds(start, blk), cols[h]], preferred_element_type=F32)
        for h in heads:
            carry_ref[h, rows, :] = carries[h]
            acc_ref[rows, cols[h]] = accs[h]
        top = functools.reduce(jnp.maximum, carries)
        return jnp.max(top)

    carry_ref[...] = jnp.zeros_like(carry_ref)
    acc_ref[...] = jnp.zeros_like(acc_ref)
    for c in reversed(range(nsub)):
        alive = step(i * nsub + c, c * blk, True)

    def cond(state):
        j, top = state
        return jnp.logical_and(j >= 0, top > SB_DEAD)

    def body(state):
        j, _ = state
        return j - 1, step(j, 0, False)

    lax.while_loop(cond, body, (i * nsub - 1, alive))
    o_ref[...] = (acc_ref[...] * _silu(g_ref[...])).astype(o_ref.dtype)


def _suffix_matrix(blk):
    j = np.arange(blk)[:, None]
    s = np.arange(blk)[None, :]
    half = np.concatenate([(j > s).astype(np.float32), np.ones((blk, blk), np.float32)], axis=1)
    return jnp.asarray(np.concatenate([half, half], axis=0), dtype=BF16)


def _stick_breaking(p_mm, p_ew):
    s = p_mm.shape[0]
    blk, tq, hw = SB_BLOCK, SB_ROWS, SB_HEADS * HEAD_DIM
    hb = WIDTH // hw
    return pl.pallas_call(
        _sb_kernel,
        grid=(hb, s // tq),
        in_specs=[
            pl.BlockSpec((2 * blk, 2 * blk), lambda h, i: (0, 0)),
            pl.BlockSpec((tq, hw), lambda h, i: (i, MM_Q * hb + h)),
            pl.BlockSpec((s, hw), lambda h, i: (0, MM_K * hb + h)),
            pl.BlockSpec((s, hw), lambda h, i: (0, MM_V * hb + h)),
            pl.BlockSpec((tq, hw), lambda h, i: (i, EW_AG * hb + h)),
        ],
        out_specs=pl.BlockSpec((tq, hw), lambda h, i: (i, h)),
        out_shape=jax.ShapeDtypeStruct((s, WIDTH), BF16),
        scratch_shapes=[pltpu.VMEM((SB_HEADS, tq, blk), F32), pltpu.VMEM((tq, hw), F32)],
        compiler_params=_cparams("parallel", "arbitrary"),
        name="sb_attn",
    )(_suffix_matrix(blk), p_mm, p_mm, p_mm, p_ew)


CONV_ROWS = 16
CONV_STAGE_ROWS = 64
CONV_SHIFT_ROWS = 40
SUBLANES = 8


def _conv_kernel(a_ref, b_ref, pa_ref, pb_ref, g_ref, w_ref, cb_ref, lng_ref, lnb_ref, o_ref, gs_ref, wb_ref, *, ts):
    i = pl.program_id(0)
    sub = SUBLANES
    halo = pa_ref[...] * _sigmoid(pb_ref[...])
    gs_ref[0, 0:CONV_HALO, :] = jnp.where(i > 0, halo, 0.0)
    for c in range(ts // CONV_STAGE_ROWS):
        rows = slice(c * CONV_STAGE_ROWS, (c + 1) * CONV_STAGE_ROWS)
        gs_ref[0, CONV_HALO + c * CONV_STAGE_ROWS:CONV_HALO + (c + 1) * CONV_STAGE_ROWS, :] = (
            a_ref[rows, :] * _sigmoid(b_ref[rows, :]))
    span = CONV_HALO + ts - sub
    for s in range(1, sub):
        for c in range(span // CONV_SHIFT_ROWS):
            r = c * CONV_SHIFT_ROWS
            gs_ref[s, r:r + CONV_SHIFT_ROWS, :] = gs_ref[0, r + s:r + s + CONV_SHIFT_ROWS, :]
    for k in range(CONV_KERNEL):
        wb_ref[k] = jnp.broadcast_to(w_ref[k:k + 1, :], (CONV_ROWS, WIDTH))
    first = CONV_HALO - (CONV_KERNEL - 1)
    bias = cb_ref[...]
    lng = lng_ref[...]
    lnb = lnb_ref[...]
    for c in range(ts // CONV_ROWS):
        base = c * CONV_ROWS
        acc = None
        for k in range(CONV_KERNEL):
            phase = (first + k) % sub
            r = base + first + k - phase
            term = gs_ref[phase, r:r + CONV_ROWS, :] * wb_ref[k]
            acc = term if acc is None else acc + term
        y = _silu(_layernorm_rows(acc + bias, lng, lnb))
        out = y * _silu(g_ref[base:base + CONV_ROWS, :])
        o_ref[base:base + CONV_ROWS, :] = out.astype(o_ref.dtype)


def _conformer_conv(p_ew, conv_w, conv_b, ln_g, ln_b, layer, *, ts=256):
    s = p_ew.shape[0]
    hb = ts // CONV_HALO
    row = lambda i: (i, 0)
    par = lambda i: (layer, 0, 0)
    return pl.pallas_call(
        functools.partial(_conv_kernel, ts=ts),
        grid=(s // ts,),
        in_specs=[
            pl.BlockSpec((ts, WIDTH), lambda i: (i, EW_BA)),
            pl.BlockSpec((ts, WIDTH), lambda i: (i, EW_BB)),
            pl.BlockSpec((CONV_HALO, WIDTH), lambda i: (jnp.maximum(i * hb - 1, 0), EW_BA)),
            pl.BlockSpec((CONV_HALO, WIDTH), lambda i: (jnp.maximum(i * hb - 1, 0), EW_BB)),
            pl.BlockSpec((ts, WIDTH), lambda i: (i, EW_BG)),
            pl.BlockSpec((None, CONV_KERNEL, WIDTH), par),
            pl.BlockSpec((None, 1, WIDTH), par),
            pl.BlockSpec((None, 1, WIDTH), par),
            pl.BlockSpec((None, 1, WIDTH), par),
        ],
        out_specs=pl.BlockSpec((ts, WIDTH), row),
        out_shape=jax.ShapeDtypeStruct((s, WIDTH), BF16),
        scratch_shapes=[pltpu.VMEM((SUBLANES, CONV_HALO + ts, WIDTH), F32),
                        pltpu.VMEM((CONV_KERNEL, CONV_ROWS, WIDTH), F32)],
        compiler_params=_cparams("parallel"),
        name="conformer_conv",
    )(p_ew, p_ew, p_ew, p_ew, p_ew, conv_w, conv_b, ln_g, ln_b)


def _sgu_kernel(u_ref, v_ref, g_ref, w_ref, bs_ref, lng_ref, lnb_ref, o_ref, *, ts):
    ck = SGU_CHUNK
    gw = WIDTH // SGU_GROUPS
    row = lax.broadcasted_iota(jnp.int32, (ck, ck), 0)
    col = lax.broadcasted_iota(jnp.int32, (ck, ck), 1)
    tril = col <= row
    lng = lng_ref[...]
    lnb = lnb_ref[...]
    ws = [jnp.where(tril, w_ref[g], 0.0).astype(BF16) for g in range(SGU_GROUPS)]
    for c in range(ts // ck):
        rows = slice(c * ck, (c + 1) * ck)
        vn = _layernorm_rows(_gelu_tanh(v_ref[rows, :]), lng, lnb).astype(BF16)
        for g in range(SGU_GROUPS):
            cols = slice(g * gw, (g + 1) * gw)
            z = jnp.dot(ws[g], vn[:, cols], preferred_element_type=F32) + bs_ref[:, cols]
            out = _gelu_tanh(u_ref[rows, cols]) * z * _silu(g_ref[rows, cols])
            o_ref[rows, cols] = out.astype(o_ref.dtype)


def _spatial_gating(p_ew, sgu_w, bs_full, ln_g, ln_b, layer, *, ts=256):
    s = p_ew.shape[0]
    par = lambda i: (layer, 0, 0)
    return pl.pallas_call(
        functools.partial(_sgu_kernel, ts=ts),
        grid=(s // ts,),
        in_specs=[
            pl.BlockSpec((ts, WIDTH), lambda i: (i, EW_CU)),
            pl.BlockSpec((ts, WIDTH), lambda i: (i, EW_CV)),
            pl.BlockSpec((ts, WIDTH), lambda i: (i, EW_CG)),
            pl.BlockSpec((None, SGU_GROUPS, SGU_CHUNK, SGU_CHUNK), lambda i: (layer, 0, 0, 0)),
            pl.BlockSpec((None, SGU_CHUNK, WIDTH), par),
            pl.BlockSpec((None, 1, WIDTH), par),
            pl.BlockSpec((None, 1, WIDTH), par),
        ],
        out_specs=pl.BlockSpec((ts, WIDTH), lambda i: (i, 0)),
        out_shape=jax.ShapeDtypeStruct((s, WIDTH), BF16),
        compiler_params=_cparams("parallel"),
        name="spatial_gating",
    )(p_ew, p_ew, p_ew, sgu_w, bs_full, ln_g, ln_b)


DIL_SUPER = 2048
DIL_COMBINE_ROWS = 256
DIL_UNROLL = 8
DIL_PARTS = 1


DIL_FREE_STRIDE = 4


def _class_reader(src_ref, start, count, dil, tmp_ref):
    if dil == 1:
        return lambda r: src_ref[start:start + count, :]
    if dil <= DIL_FREE_STRIDE:
        return lambda r: src_ref[pl.ds(start + r, count, stride=dil), :]
    f = DIL_FREE_STRIDE
    inner = dil // f
    assert inner <= f and dil % f == 0
    q = count * inner
    for a in range(f):
        tmp_ref[a * q:(a + 1) * q, :] = src_ref[pl.ds(start + a, q, stride=f), :]
    return lambda r: tmp_ref[pl.ds((r % f) * q + r // f, count, stride=inner), :]


def _classes_to_positions(src_ref, g, pos_ref, slot, dil, rows, tmp_ref):
    n = rows // dil
    if dil <= DIL_FREE_STRIDE:
        for r in range(dil):
            pos_ref[slot, pl.ds(r, n, stride=dil), :] = src_ref[g, r * n:(r + 1) * n, :]
        return
    f = DIL_FREE_STRIDE
    inner = dil // f
    q = n * inner
    for r in range(dil):
        tmp_ref[pl.ds((r % f) * q + r // f, n, stride=inner), :] = src_ref[g, r * n:(r + 1) * n, :]
    for a in range(f):
        pos_ref[slot, pl.ds(a, q, stride=f), :] = tmp_ref[a * q:(a + 1) * q, :]


def _dil_kernel(q0_ref, q1_ref, q2_ref, k0_ref, k1_ref, k2_ref, kp0_ref, kp1_ref, kp2_ref, v_ref, vp_ref, g_ref,
                o_ref, qs_ref, ks_ref, vs_ref, out_ref, lse_ref, pos_ref, tmp_ref):
    t, sup = DIL_BLOCK, DIL_SUPER
    i = pl.program_id(0)
    q_refs = (q0_ref, q1_ref, q2_ref)
    k_refs = (k0_ref, k1_ref, k2_ref)
    kp_refs = (kp0_ref, kp1_ref, kp2_ref)
    scale = HEAD_DIM ** -0.5
    pr = t // DIL_PARTS
    row = lax.broadcasted_iota(jnp.int32, (pr, pr + t), 0)
    col = lax.broadcasted_iota(jnp.int32, (pr, pr + t), 1)
    band = jnp.logical_and(col >= row, col <= row + t)

    for g, (_, dil) in enumerate(DIL_PATTERNS):
        n = sup // dil
        sources = (
            (q_refs[g], 0, n, lambda r: (qs_ref, r * n)),
            (kp_refs[g], 0, t, lambda r: (ks_ref, r * (n + t))),
            (k_refs[g], 0, n, lambda r: (ks_ref, r * (n + t) + t)),
            (vp_ref, sup - t * dil, t, lambda r: (vs_ref, r * (n + t))),
            (v_ref, 0, n, lambda r: (vs_ref, r * (n + t) + t)),
        )
        for src_ref, start, count, dest in sources:
            read = _class_reader(src_ref, start, count, dil, tmp_ref)
            for r in range(dil):
                dst_ref, off = dest(r)
                dst_ref[g, off:off + count, :] = read(r).astype(BF16)

        per_class = n // t

        def block(u, carry, g=g, per_class=per_class):
            r = u // per_class
            bb = u - r * per_class
            has_prev = jnp.logical_or(i > 0, bb > 0)
            for part in range(DIL_PARTS):
                qoff = pl.multiple_of(u * t + part * pr, pr)
                koff = pl.multiple_of((u + r) * t + part * pr, pr)
                q = qs_ref[g, pl.ds(qoff, pr), :]
                k = ks_ref[g, pl.ds(koff, pr + t), :]
                v = vs_ref[g, pl.ds(koff, pr + t), :]
                s = lax.dot_general(q, k, (((1,), (1,)), ((), ())), preferred_element_type=F32) * scale
                valid = jnp.logical_and(band, jnp.logical_or(col >= t - part * pr, has_prev))
                s = jnp.where(valid, s, NEG_BIG)
                m = jnp.max(s, axis=-1, keepdims=True)
                p = jnp.exp(s - m)
                den = jnp.sum(p, axis=-1, keepdims=True)
                out_ref[g, pl.ds(qoff, pr), :] = jnp.dot(p.astype(BF16), v, preferred_element_type=F32) * (1.0 / den)
                lse_ref[g, pl.ds(qoff, pr), :] = jnp.broadcast_to(m + jnp.log(den), (pr, HEAD_DIM))
            return carry

        lax.fori_loop(0, sup // t, block, 0, unroll=DIL_UNROLL)

    slot = 0
    for g in (1, 2):
        for src_ref in (out_ref, lse_ref):
            _classes_to_positions(src_ref, g, pos_ref, slot, DIL_PATTERNS[g][1], sup, tmp_ref)
            slot += 1

    cr = DIL_COMBINE_ROWS
    for c in range(sup // cr):
        rows = slice(c * cr, (c + 1) * cr)
        l0, l1, l2 = lse_ref[0, rows, :], pos_ref[1, rows, :], pos_ref[3, rows, :]
        m = jnp.maximum(jnp.maximum(l0, l1), l2)
        e0, e1, e2 = jnp.exp(l0 - m), jnp.exp(l1 - m), jnp.exp(l2 - m)
        mix = e0 * out_ref[0, rows, :] + e1 * pos_ref[0, rows, :] + e2 * pos_ref[2, rows, :]
        o_ref[rows, :] = ((mix / (e0 + e1 + e2)) * _silu(g_ref[rows, :])).astype(o_ref.dtype)


def _dilated_mixture(p_ew):
    s = p_ew.shape[0]
    t, sup = DIL_BLOCK, DIL_SUPER
    assert all(w // d == t for w, d in DIL_PATTERNS) and sup == t * DIL_PATTERNS[-1][1]
    nh = N_HEADS
    cur = lambda blk: (lambda i, h: (i, blk * nh + h))
    specs = [pl.BlockSpec((sup, HEAD_DIM), cur(EW_DQ + g)) for g in range(3)]
    specs += [pl.BlockSpec((sup, HEAD_DIM), cur(EW_DK + g)) for g in range(3)]
    for g, (_, dil) in enumerate(DIL_PATTERNS):
        rows = t * dil
        per = sup // rows
        specs.append(pl.BlockSpec((rows, HEAD_DIM),
                                  lambda i, h, g=g, per=per: (jnp.maximum(i * per - 1, 0), (EW_DK + g) * nh + h)))
    specs.append(pl.BlockSpec((sup, HEAD_DIM), cur(EW_DV)))
    specs.append(pl.BlockSpec((sup, HEAD_DIM), lambda i, h: (jnp.maximum(i - 1, 0), EW_DV * nh + h)))
    specs.append(pl.BlockSpec((sup, HEAD_DIM), cur(EW_DG)))
    hist = max(sup + t * d for _, d in DIL_PATTERNS)
    return pl.pallas_call(
        _dil_kernel,
        grid=(s // sup, nh),
        in_specs=specs,
        out_specs=pl.BlockSpec((sup, HEAD_DIM), lambda i, h: (i, h)),
        out_shape=jax.ShapeDtypeStruct((s, WIDTH), BF16),
        scratch_shapes=[
            pltpu.VMEM((3, sup, HEAD_DIM), BF16),
            pltpu.VMEM((3, hist, HEAD_DIM), BF16),
            pltpu.VMEM((3, hist, HEAD_DIM), BF16),
            pltpu.VMEM((3, sup, HEAD_DIM), F32),
            pltpu.VMEM((3, sup, HEAD_DIM), F32),
            pltpu.VMEM((4, sup, HEAD_DIM), F32),
            pltpu.VMEM((sup, HEAD_DIM), F32),
        ],
        compiler_params=_cparams("parallel", "parallel"),
        name="dilated_attn",
    )(*([p_ew] * 12))


def _gate_merge_kernel(h_ref, ya_ref, yb_ref, yc_ref, yd_ref, wg0_ref, wg1_ref, wg2_ref, wg3_ref,
                       bg_ref, wb_ref, o_ref):
    h = h_ref[...]
    merged = None
    branches = ((ya_ref, wg0_ref), (yb_ref, wg1_ref), (yc_ref, wg2_ref), (yd_ref, wg3_ref))
    for n, (y_ref, wg_ref) in enumerate(branches):
        gate = _sigmoid(jnp.dot(h, wg_ref[...], preferred_element_type=F32) + bg_ref[n:n + 1, :])
        term = gate * jnp.dot(y_ref[...], wb_ref[n], preferred_element_type=F32)
        merged = term if merged is None else merged + term
    o_ref[...] = merged.astype(o_ref.dtype)


def _gate_merge(h, ys, w_gate, b_gate, w_branch, layer, *, tm=1024, tn=256):
    s, d = h.shape
    nj = d // tn
    y_spec = pl.BlockSpec((tm, WIDTH), lambda i, j: (i, 0))
    wg_spec = lambda n: pl.BlockSpec((None, d, tn), lambda i, j: (layer, 0, n * nj + j))
    return pl.pallas_call(
        _gate_merge_kernel,
        grid=(s // tm, nj),
        in_specs=[
            pl.BlockSpec((tm, d), lambda i, j: (i, 0)),
            y_spec, y_spec, y_spec, y_spec,
            wg_spec(0), wg_spec(1), wg_spec(2), wg_spec(3),
            pl.BlockSpec((None, N_BRANCH, tn), lambda i, j: (layer, 0, j)),
            pl.BlockSpec((None, N_BRANCH, WIDTH, tn), lambda i, j: (layer, 0, 0, j)),
        ],
        out_specs=pl.BlockSpec((tm, tn), lambda i, j: (i, j)),
        out_shape=jax.ShapeDtypeStruct((s, d), BF16),
        compiler_params=_cparams("parallel", "parallel"),
        name="gate_merge",
    )(h, *ys, w_gate, w_gate, w_gate, w_gate, b_gate, w_branch)


def _out_proj_kernel(m_ref, x_ref, gn_ref, wo_ref, o_ref, hn_ref, *, tm):
    o_ref[...] = x_ref[...] + jnp.dot(m_ref[...], wo_ref[...], preferred_element_type=F32)
    _norm_into(o_ref, gn_ref, hn_ref, tm)


def _out_proj(merged, x, next_g, w_out, layer, norm_dtype, *, tm=512):
    s, d = x.shape
    row_spec = pl.BlockSpec((tm, d), lambda i: (i, 0))
    return pl.pallas_call(
        functools.partial(_out_proj_kernel, tm=tm),
        grid=(s // tm,),
        in_specs=[
            row_spec, row_spec,
            pl.BlockSpec((None, 1, d), lambda i: (layer, 0, 0)),
            pl.BlockSpec((None, d, d), lambda i: (layer, 0, 0)),
        ],
        out_specs=[row_spec, row_spec],
        out_shape=[jax.ShapeDtypeStruct((s, d), F32), jax.ShapeDtypeStruct((s, d), norm_dtype)],
        compiler_params=_cparams("parallel"),
        name="out_proj",
    )(merged, x, next_g, w_out)


def kernel(x, norm_g, w_in, conv_w, conv_b, conv_ln_g, conv_ln_b, sgu_ln_g, sgu_ln_b, sgu_w, sgu_b, w_branch, w_gate, b_gate, w_out, final_g):
    batch, s, d = x.shape
    assert batch == 1 and d == D_MODEL
    depth = norm_g.shape[0]
    w_gate_b = w_gate.astype(BF16)
    w_branch_b = w_branch.astype(BF16)
    w_out_b = w_out.astype(BF16)
    as_rows = lambda a: a.reshape(depth, 1, a.shape[-1])
    norm_g3, conv_b3 = as_rows(norm_g), as_rows(conv_b)
    conv_ln_g3, conv_ln_b3 = as_rows(conv_ln_g), as_rows(conv_ln_b)
    sgu_ln_g3, sgu_ln_b3 = as_rows(sgu_ln_g), as_rows(sgu_ln_b)
    b_gate3 = b_gate.reshape(depth, N_BRANCH, d)
    bs_full = jnp.repeat(jnp.swapaxes(sgu_b, 1, 2), WIDTH // SGU_GROUPS, axis=2)

    next_g3 = jnp.concatenate([norm_g3[1:], final_g.reshape(1, 1, d)], axis=0)

    xs = x.reshape(s, d)
    h = _first_norm(xs, norm_g3)
    for l in range(depth):
        p_mm, p_ew = _project(h, w_in, l)
        ya = _stick_breaking(p_mm, p_ew)
        yb = _conformer_conv(p_ew, conv_w, conv_b3, conv_ln_g3, conv_ln_b3, l)
        yc = _spatial_gating(p_ew, sgu_w, bs_full, sgu_ln_g3, sgu_ln_b3, l)
        yd = _dilated_mixture(p_ew)
        norm_dtype = F32 if l == depth - 1 else BF16
        merged = _gate_merge(h, (ya, yb, yc, yd), w_gate_b, b_gate3, w_branch_b, l)
        xs, h = _out_proj(merged, xs, next_g3, w_out_b, l, norm_dtype)
    return h.reshape(batch, s, d)
```

```python
import functools

import numpy as np
import jax
import jax.numpy as jnp
from jax import lax
from jax.experimental import pallas as pl
from jax.experimental.pallas import tpu as pltpu

F32 = jnp.float32
BF16 = jnp.bfloat16

D_MODEL = 2048
HEAD_DIM = 128
NORM_EPS = 1e-6
N_HEADS = 8
WIDTH = N_HEADS * HEAD_DIM
CONV_KERNEL = 31
CONV_HALO = 32
SGU_GROUPS = 8
SGU_CHUNK = 128
DIL_PATTERNS = ((128, 1), (512, 4), (2048, 16))
DIL_BLOCK = 128
N_BRANCH = 4
IN_WIDTH = 18432

MM_WIDTH = 3072
EW_WIDTH = IN_WIDTH - MM_WIDTH
MM_Q, MM_K, MM_V = 0, 1, 2
EW_AG, EW_BA, EW_BB, EW_BG, EW_CU, EW_CV, EW_CG, EW_DQ, EW_DK, EW_DV, EW_DG = 0, 1, 2, 3, 4, 5, 6, 7, 10, 13, 14

SB_DEAD = -104.0
NEG_BIG = -1e30
VMEM_LIMIT = 56 * 1024 * 1024


def _cparams(*sem):
    return pltpu.CompilerParams(dimension_semantics=sem, vmem_limit_bytes=VMEM_LIMIT)


def _sigmoid(x):
    return 1.0 / (1.0 + jnp.exp(-x))


def _silu(x):
    return x * _sigmoid(x)


def _gelu_tanh(x):
    c = np.sqrt(2.0 / np.pi).astype(np.float32)
    return x * (0.5 * (1.0 + jnp.tanh(c * (x + 0.044715 * (x * x * x)))))


def _rmsnorm_rows(x, g):
    ms = jnp.mean(x * x, axis=-1, keepdims=True)
    return x * lax.rsqrt(ms + NORM_EPS) * g


def _layernorm_rows(x, g, b):
    mu = jnp.mean(x, axis=-1, keepdims=True)
    xc = x - mu
    var = jnp.mean(xc * xc, axis=-1, keepdims=True)
    return xc * lax.rsqrt(var + NORM_EPS) * g + b


NORM_ROWS = 256


def _norm_into(x_ref, g_ref, h_ref, rows):
    g = g_ref[...]
    for c in range(rows // NORM_ROWS):
        sl = slice(c * NORM_ROWS, (c + 1) * NORM_ROWS)
        h_ref[sl, :] = _rmsnorm_rows(x_ref[sl, :], g).astype(h_ref.dtype)


def _first_norm_kernel(x_ref, g_ref, h_ref, *, ts):
    _norm_into(x_ref, g_ref, h_ref, ts)


def _first_norm(x, norm_g, *, ts=512):
    s, d = x.shape
    return pl.pallas_call(
        functools.partial(_first_norm_kernel, ts=ts),
        grid=(s // ts,),
        in_specs=[pl.BlockSpec((ts, d), lambda i: (i, 0)), pl.BlockSpec((None, 1, d), lambda i: (0, 0, 0))],
        out_specs=pl.BlockSpec((ts, d), lambda i: (i, 0)),
        out_shape=jax.ShapeDtypeStruct((s, d), BF16),
        compiler_params=_cparams("parallel"),
        name="first_norm",
    )(x, norm_g)


CAST_ROWS = 256


def _proj_kernel(h_ref, w_ref, o_ref, wb_ref):
    @pl.when(pl.program_id(1) == 0)
    def _():
        for c in range(w_ref.shape[0] // CAST_ROWS):
            sl = slice(c * CAST_ROWS, (c + 1) * CAST_ROWS)
            wb_ref[sl, :] = w_ref[sl, :].astype(BF16)

    o_ref[...] = jnp.dot(h_ref[...], wb_ref[...], preferred_element_type=F32).astype(o_ref.dtype)


def _project_cols(h, w_in, layer, col0, width, out_dtype, *, tm=1024, tn=1024):
    s, d = h.shape
    j0 = col0 // tn
    return pl.pallas_call(
        _proj_kernel,
        grid=(width // tn, s // tm),
        in_specs=[
            pl.BlockSpec((tm, d), lambda j, i: (i, 0)),
            pl.BlockSpec((None, d, tn), lambda j, i: (layer, 0, j0 + j)),
        ],
        out_specs=pl.BlockSpec((tm, tn), lambda j, i: (i, j)),
        out_shape=jax.ShapeDtypeStruct((s, width), out_dtype),
        scratch_shapes=[pltpu.VMEM((d, tn), BF16)],
        compiler_params=_cparams("arbitrary", "arbitrary"),
        name="in_proj",
    )(h, w_in)


def _project(h, w_in, layer):
    p_mm = _project_cols(h, w_in, layer, 0, MM_WIDTH, BF16)
    p_ew = _project_cols(h, w_in, layer, MM_WIDTH, EW_WIDTH, F32)
    return p_mm, p_ew


SB_BLOCK = 128
SB_ROWS = 256
SB_HEADS = 4


def _sb_kernel(uu_ref, q_ref, k_ref, v_ref, g_ref, o_ref, carry_ref, acc_ref):
    blk, tq = SB_BLOCK, SB_ROWS
    nsub = tq // blk
    i = pl.program_id(1)
    uu = uu_ref[...]
    scale = HEAD_DIM ** -0.5

    def step(j, r0, masked):
        start = pl.multiple_of(j * blk, blk)
        rows = slice(r0, tq)
        if masked:
            row = lax.broadcasted_iota(jnp.int32, (tq - r0, blk), 0)
            col = lax.broadcasted_iota(jnp.int32, (tq - r0, blk), 1)
            mask = col < row
        heads = range(SB_HEADS)
        cols = [slice(h * HEAD_DIM, (h + 1) * HEAD_DIM) for h in heads]
        carries = [carry_ref[h, rows, :] for h in heads]
        accs = [acc_ref[rows, cols[h]] for h in heads]
        nt = (((1,), (1,)), ((), ()))
        zs = [lax.dot_general(q_ref[rows, cols[h]], k_ref[pl.ds(start, blk), cols[h]], nt,
                              preferred_element_type=F32) * scale for h in heads]
        lbetas, splits = [], []
        for h in heads:
            sp = jnp.maximum(zs[h], 0.0) + jnp.log(1.0 + jnp.exp(-jnp.abs(zs[h])))
            l1mb = -sp
            if masked:
                l1mb = jnp.where(mask, l1mb, 0.0)
            hi = l1mb.astype(BF16)
            lo = (l1mb - hi.astype(F32)).astype(BF16)
            lbetas.append(zs[h] - sp)
            splits.append(jnp.concatenate([hi, lo], axis=1))
        rs = [jnp.dot(splits[h], uu, preferred_element_type=F32) for h in heads]
        ps = []
        for h in heads:
            p = jnp.exp(lbetas[h] + (carries[h] + rs[h][:, :blk]))
            if masked:
                p = jnp.where(mask, p, 0.0)
            ps.append(p.astype(BF16))
            carries[h] = carries[h] + rs[h][:, blk:]
        for h in heads:
            accs[h] = accs[h] + jnp.dot(ps[h], v_ref[pl.ds(start, blk), cols[h]], preferred_element_type=F32)
        for h in heads:
            carry_ref[h, rows, :] = carries[h]
            acc_ref[rows, cols[h]] = accs[h]
        top = functools.reduce(jnp.maximum, carries)
        return jnp.max(top)

    carry_ref[...] = jnp.zeros_like(carry_ref)
    acc_ref[...] = jnp.zeros_like(acc_ref)
    for c in reversed(range(nsub)):
        alive = step(i * nsub + c, c * blk, True)

    def cond(state):
        j, top = state
        return jnp.logical_and(j >= 0, top > SB_DEAD)

    def body(state):
        j, _ = state
        return j - 1, step(j, 0, False)

    lax.while_loop(cond, body, (i * nsub - 1, alive))
    o_ref[...] = (acc_ref[...] * _silu(g_ref[...])).astype(o_ref.dtype)


def _suffix_matrix(blk):
    j = np.arange(blk)[:, None]
    s = np.arange(blk)[None, :]
    half = np.concatenate([(j > s).astype(np.float32), np.ones((blk, blk), np.float32)], axis=1)
    return jnp.asarray(np.concatenate([half, half], axis=0), dtype=BF16)


def _stick_breaking(p_mm, p_ew):
    s = p_mm.shape[0]
    blk, tq, hw = SB_BLOCK, SB_ROWS, SB_HEADS * HEAD_DIM
    hb = WIDTH // hw
    return pl.pallas_call(
        _sb_kernel,
        grid=(hb, s // tq),
        in_specs=[
            pl.BlockSpec((2 * blk, 2 * blk), lambda h, i: (0, 0)),
            pl.BlockSpec((tq, hw), lambda h, i: (i, MM_Q * hb + h)),
            pl.BlockSpec((s, hw), lambda h, i: (0, MM_K * hb + h)),
            pl.BlockSpec((s, hw), lambda h, i: (0, MM_V * hb + h)),
            pl.BlockSpec((tq, hw), lambda h, i: (i, EW_AG * hb + h)),
        ],
        out_specs=pl.BlockSpec((tq, hw), lambda h, i: (i, h)),
        out_shape=jax.ShapeDtypeStruct((s, WIDTH), BF16),
        scratch_shapes=[pltpu.VMEM((SB_HEADS, tq, blk), F32), pltpu.VMEM((tq, hw), F32)],
        compiler_params=_cparams("parallel", "arbitrary"),
        name="sb_attn",
    )(_suffix_matrix(blk), p_mm, p_mm, p_mm, p_ew)


CONV_ROWS = 16
CONV_STAGE_ROWS = 64
CONV_SHIFT_ROWS = 40
SUBLANES = 8


def _conv_kernel(a_ref, b_ref, pa_ref, pb_ref, g_ref, w_ref, cb_ref, lng_ref, lnb_ref, o_ref, gs_ref, wb_ref, *, ts):
    i = pl.program_id(0)
    sub = SUBLANES
    halo = pa_ref[...] * _sigmoid(pb_ref[...])
    gs_ref[0, 0:CONV_HALO, :] = jnp.where(i > 0, halo, 0.0)
    for c in range(ts // CONV_STAGE_ROWS):
        rows = slice(c * CONV_STAGE_ROWS, (c + 1) * CONV_STAGE_ROWS)
        gs_ref[0, CONV_HALO + c * CONV_STAGE_ROWS:CONV_HALO + (c + 1) * CONV_STAGE_ROWS, :] = (
            a_ref[rows, :] * _sigmoid(b_ref[rows, :]))
    span = CONV_HALO + ts - sub
    for s in range(1, sub):
        for c in range(span // CONV_SHIFT_ROWS):
            r = c * CONV_SHIFT_ROWS
            gs_ref[s, r:r + CONV_SHIFT_ROWS, :] = gs_ref[0, r + s:r + s + CONV_SHIFT_ROWS, :]
    for k in range(CONV_KERNEL):
        wb_ref[k] = jnp.broadcast_to(w_ref[k:k + 1, :], (CONV_ROWS, WIDTH))
    first = CONV_HALO - (CONV_KERNEL - 1)
    bias = cb_ref[...]
    lng = lng_ref[...]
    lnb = lnb_ref[...]
    for c in range(ts // CONV_ROWS):
        base = c * CONV_ROWS
        acc = None
        for k in range(CONV_KERNEL):
            phase = (first + k) % sub
            r = base + first + k - phase
            term = gs_ref[phase, r:r + CONV_ROWS, :] * wb_ref[k]
            acc = term if acc is None else acc + term
        y = _silu(_layernorm_rows(acc + bias, lng, lnb))
        out = y * _silu(g_ref[base:base + CONV_ROWS, :])
        o_ref[base:base + CONV_ROWS, :] = out.astype(o_ref.dtype)


def _conformer_conv(p_ew, conv_w, conv_b, ln_g, ln_b, layer, *, ts=256):
    s = p_ew.shape[0]
    hb = ts // CONV_HALO
    row = lambda i: (i, 0)
    par = lambda i: (layer, 0, 0)
    return pl.pallas_call(
        functools.partial(_conv_kernel, ts=ts),
        grid=(s // ts,),
        in_specs=[
            pl.BlockSpec((ts, WIDTH), lambda i: (i, EW_BA)),
            pl.BlockSpec((ts, WIDTH), lambda i: (i, EW_BB)),
            pl.BlockSpec((CONV_HALO, WIDTH), lambda i: (jnp.maximum(i * hb - 1, 0), EW_BA)),
            pl.BlockSpec((CONV_HALO, WIDTH), lambda i: (jnp.maximum(i * hb - 1, 0), EW_BB)),
            pl.BlockSpec((ts, WIDTH), lambda i: (i, EW_BG)),
            pl.BlockSpec((None, CONV_KERNEL, WIDTH), par),
            pl.BlockSpec((None, 1, WIDTH), par),
            pl.BlockSpec((None, 1, WIDTH), par),
            pl.BlockSpec((None, 1, WIDTH), par),
        ],
        out_specs=pl.BlockSpec((ts, WIDTH), row),
        out_shape=jax.ShapeDtypeStruct((s, WIDTH), BF16),
        scratch_shapes=[pltpu.VMEM((SUBLANES, CONV_HALO + ts, WIDTH), F32),
                        pltpu.VMEM((CONV_KERNEL, CONV_ROWS, WIDTH), F32)],
        compiler_params=_cparams("parallel"),
        name="conformer_conv",
    )(p_ew, p_ew, p_ew, p_ew, p_ew, conv_w, conv_b, ln_g, ln_b)


def _sgu_kernel(u_ref, v_ref, g_ref, w_ref, bs_ref, lng_ref, lnb_ref, o_ref, *, ts):
    ck = SGU_CHUNK
    gw = WIDTH // SGU_GROUPS
    row = lax.broadcasted_iota(jnp.int32, (ck, ck), 0)
    col = lax.broadcasted_iota(jnp.int32, (ck, ck), 1)
    tril = col <= row
    lng = lng_ref[...]
    lnb = lnb_ref[...]
    ws = [jnp.where(tril, w_ref[g], 0.0).astype(BF16) for g in range(SGU_GROUPS)]
    for c in range(ts // ck):
        rows = slice(c * ck, (c + 1) * ck)
        vn = _layernorm_rows(_gelu_tanh(v_ref[rows, :]), lng, lnb).astype(BF16)
        for g in range(SGU_GROUPS):
            cols = slice(g * gw, (g + 1) * gw)
            z = jnp.dot(ws[g], vn[:, cols], preferred_element_type=F32) + bs_ref[:, cols]
            out = _gelu_tanh(u_ref[rows, cols]) * z * _silu(g_ref[rows, cols])
            o_ref[rows, cols] = out.astype(o_ref.dtype)


def _spatial_gating(p_ew, sgu_w, bs_full, ln_g, ln_b, layer, *, ts=256):
    s = p_ew.shape[0]
    par = lambda i: (layer, 0, 0)
    return pl.pallas_call(
        functools.partial(_sgu_kernel, ts=ts),
        grid=(s // ts,),
        in_specs=[
            pl.BlockSpec((ts, WIDTH), lambda i: (i, EW_CU)),
            pl.BlockSpec((ts, WIDTH), lambda i: (i, EW_CV)),
            pl.BlockSpec((ts, WIDTH), lambda i: (i, EW_CG)),
            pl.BlockSpec((None, SGU_GROUPS, SGU_CHUNK, SGU_CHUNK), lambda i: (layer, 0, 0, 0)),
            pl.BlockSpec((None, SGU_CHUNK, WIDTH), par),
            pl.BlockSpec((None, 1, WIDTH), par),
            pl.BlockSpec((None, 1, WIDTH), par),
        ],
        out_specs=pl.BlockSpec((ts, WIDTH), lambda i: (i, 0)),
        out_shape=jax.ShapeDtypeStruct((s, WIDTH), BF16),
        compiler_params=_cparams("parallel"),
        name="spatial_gating",
    )(p_ew, p_ew, p_ew, sgu_w, bs_full, ln_g, ln_b)


DIL_SUPER = 2048
DIL_COMBINE_ROWS = 256
DIL_UNROLL = 8


DIL_FREE_STRIDE = 4


def _class_reader(src_ref, start, count, dil, tmp_ref):
    if dil == 1:
        return lambda r: src_ref[start:start + count, :]
    if dil <= DIL_FREE_STRIDE:
        return lambda r: src_ref[pl.ds(start + r, count, stride=dil), :]
    f = DIL_FREE_STRIDE
    inner = dil // f
    assert inner <= f and dil % f == 0
    q = count * inner
    for a in range(f):
        tmp_ref[a * q:(a + 1) * q, :] = src_ref[pl.ds(start + a, q, stride=f), :]
    return lambda r: tmp_ref[pl.ds((r % f) * q + r // f, count, stride=inner), :]


def _classes_to_positions(src_ref, g, pos_ref, slot, dil, rows, tmp_ref):
    n = rows // dil
    if dil <= DIL_FREE_STRIDE:
        for r in range(dil):
            pos_ref[slot, pl.ds(r, n, stride=dil), :] = src_ref[g, r * n:(r + 1) * n, :]
        return
    f = DIL_FREE_STRIDE
    inner = dil // f
    q = n * inner
    for r in range(dil):
        tmp_ref[pl.ds((r % f) * q + r // f, n, stride=inner), :] = src_ref[g, r * n:(r + 1) * n, :]
    for a in range(f):
        pos_ref[slot, pl.ds(a, q, stride=f), :] = tmp_ref[a * q:(a + 1) * q, :]


def _dil_kernel(q0_ref, q1_ref, q2_ref, k0_ref, k1_ref, k2_ref, kp0_ref, kp1_ref, kp2_ref, v_ref, vp_ref, g_ref,
                o_ref, qs_ref, ks_ref, vs_ref, out_ref, lse_ref, pos_ref, tmp_ref):
    t, sup = DIL_BLOCK, DIL_SUPER
    i = pl.program_id(0)
    q_refs = (q0_ref, q1_ref, q2_ref)
    k_refs = (k0_ref, k1_ref, k2_ref)
    kp_refs = (kp0_ref, kp1_ref, kp2_ref)
    scale = HEAD_DIM ** -0.5
    row = lax.broadcasted_iota(jnp.int32, (t, 2 * t), 0)
    col = lax.broadcasted_iota(jnp.int32, (t, 2 * t), 1)
    band = jnp.logical_and(col >= row, col <= row + t)
    own = col >= t

    for g, (_, dil) in enumerate(DIL_PATTERNS):
        n = sup // dil
        sources = (
            (q_refs[g], 0, n, lambda r: (qs_ref, r * n)),
            (kp_refs[g], 0, t, lambda r: (ks_ref, r * (n + t))),
            (k_refs[g], 0, n, lambda r: (ks_ref, r * (n + t) + t)),
            (vp_ref, sup - t * dil, t, lambda r: (vs_ref, r * (n + t))),
            (v_ref, 0, n, lambda r: (vs_ref, r * (n + t) + t)),
        )
        for src_ref, start, count, dest in sources:
            read = _class_reader(src_ref, start, count, dil, tmp_ref)
            for r in range(dil):
                dst_ref, off = dest(r)
                dst_ref[g, off:off + count, :] = read(r).astype(BF16)

        per_class = n // t

        def blocks(c, carry, g=g, per_class=per_class):
            us = [c * DIL_UNROLL + b for b in range(DIL_UNROLL)]
            rs = [u // per_class for u in us]
            qoffs = [pl.multiple_of(u * t, t) for u in us]
            koffs = [pl.multiple_of((u + r) * t, t) for u, r in zip(us, rs)]
            nt = (((1,), (1,)), ((), ()))
            ss = [lax.dot_general(qs_ref[g, pl.ds(qo, t), :], ks_ref[g, pl.ds(ko, 2 * t), :], nt,
                                  preferred_element_type=F32) * scale for qo, ko in zip(qoffs, koffs)]
            ps, stats = [], []
            for u, r, s in zip(us, rs, ss):
                has_prev = jnp.logical_or(i > 0, u > r * per_class)
                s = jnp.where(jnp.logical_and(band, jnp.logical_or(own, has_prev)), s, NEG_BIG)
                m = jnp.max(s, axis=-1, keepdims=True)
                p = jnp.exp(s - m)
                ps.append(p.astype(BF16))
                stats.append((m, jnp.sum(p, axis=-1, keepdims=True)))
            nums = [jnp.dot(p, vs_ref[g, pl.ds(ko, 2 * t), :], preferred_element_type=F32) for p, ko in zip(ps, koffs)]
            for qo, num, (m, den) in zip(qoffs, nums, stats):
                out_ref[g, pl.ds(qo, t), :] = num * (1.0 / den)
                lse_ref[g, pl.ds(qo, t), :] = jnp.broadcast_to(m + jnp.log(den), (t, HEAD_DIM))
            return carry

        lax.fori_loop(0, sup // (t * DIL_UNROLL), blocks, 0)

    slot = 0
    for g in (1, 2):
        for src_ref in (out_ref, lse_ref):
            _classes_to_positions(src_ref, g, pos_ref, slot, DIL_PATTERNS[g][1], sup, tmp_ref)
            slot += 1

    cr = DIL_COMBINE_ROWS
    for c in range(sup // cr):
        rows = slice(c * cr, (c + 1) * cr)
        l0, l1, l2 = lse_ref[0, rows, :], pos_ref[1, rows, :], pos_ref[3, rows, :]
        m = jnp.maximum(jnp.maximum(l0, l1), l2)
        e0, e1, e2 = jnp.exp(l0 - m), jnp.exp(l1 - m), jnp.exp(l2 - m)
        mix = e0 * out_ref[0, rows, :] + e1 * pos_ref[0, rows, :] + e2 * pos_ref[2, rows, :]
        o_ref[rows, :] = ((mix / (e0 + e1 + e2)) * _silu(g_ref[rows, :])).astype(o_ref.dtype)


def _dilated_mixture(p_ew):
    s = p_ew.shape[0]
    t, sup = DIL_BLOCK, DIL_SUPER
    assert all(w // d == t for w, d in DIL_PATTERNS) and sup == t * DIL_PATTERNS[-1][1]
    nh = N_HEADS
    cur = lambda blk: (lambda i, h: (i, blk * nh + h))
    specs = [pl.BlockSpec((sup, HEAD_DIM), cur(EW_DQ + g)) for g in range(3)]
    specs += [pl.BlockSpec((sup, HEAD_DIM), cur(EW_DK + g)) for g in range(3)]
    for g, (_, dil) in enumerate(DIL_PATTERNS):
        rows = t * dil
        per = sup // rows
        specs.append(pl.BlockSpec((rows, HEAD_DIM),
                                  lambda i, h, g=g, per=per: (jnp.maximum(i * per - 1, 0), (EW_DK + g) * nh + h)))
    specs.append(pl.BlockSpec((sup, HEAD_DIM), cur(EW_DV)))
    specs.append(pl.BlockSpec((sup, HEAD_DIM), lambda i, h: (jnp.maximum(i - 1, 0), EW_DV * nh + h)))
    specs.append(pl.BlockSpec((sup, HEAD_DIM), cur(EW_DG)))
    hist = max(sup + t * d for _, d in DIL_PATTERNS)
    return pl.pallas_call(
        _dil_kernel,
        grid=(s // sup, nh),
        in_specs=specs,
        out_specs=pl.BlockSpec((sup, HEAD_DIM), lambda i, h: (i, h)),
        out_shape=jax.ShapeDtypeStruct((s, WIDTH), BF16),
        scratch_shapes=[
            pltpu.VMEM((3, sup, HEAD_DIM), BF16),
            pltpu.VMEM((3, hist, HEAD_DIM), BF16),
            pltpu.VMEM((3, hist, HEAD_DIM), BF16),
            pltpu.VMEM((3, sup, HEAD_DIM), F32),
            pltpu.VMEM((3, sup, HEAD_DIM), F32),
            pltpu.VMEM((4, sup, HEAD_DIM), F32),
            pltpu.VMEM((sup, HEAD_DIM), F32),
        ],
        compiler_params=_cparams("parallel", "parallel"),
        name="dilated_attn",
    )(*([p_ew] * 12))


def _gate_merge_kernel(h_ref, ya_ref, yb_ref, yc_ref, yd_ref, wg0_ref, wg1_ref, wg2_ref, wg3_ref,
                       bg_ref, wb_ref, o_ref):
    h = h_ref[...]
    merged = None
    branches = ((ya_ref, wg0_ref), (yb_ref, wg1_ref), (yc_ref, wg2_ref), (yd_ref, wg3_ref))
    for n, (y_ref, wg_ref) in enumerate(branches):
        gate = _sigmoid(jnp.dot(h, wg_ref[...].astype(BF16), preferred_element_type=F32) + bg_ref[n:n + 1, :])
        term = gate * jnp.dot(y_ref[...], wb_ref[n].astype(BF16), preferred_element_type=F32)
        merged = term if merged is None else merged + term
    o_ref[...] = merged.astype(o_ref.dtype)


def _gate_merge(h, ys, w_gate, b_gate, w_branch, layer, *, tm=1024, tn=256):
    s, d = h.shape
    nj = d // tn
    y_spec = pl.BlockSpec((tm, WIDTH), lambda i, j: (i, 0))
    wg_spec = lambda n: pl.BlockSpec((None, d, tn), lambda i, j: (layer, 0, n * nj + j))
    return pl.pallas_call(
        _gate_merge_kernel,
        grid=(s // tm, nj),
        in_specs=[
            pl.BlockSpec((tm, d), lambda i, j: (i, 0)),
            y_spec, y_spec, y_spec, y_spec,
            wg_spec(0), wg_spec(1), wg_spec(2), wg_spec(3),
            pl.BlockSpec((None, N_BRANCH, tn), lambda i, j: (layer, 0, j)),
            pl.BlockSpec((None, N_BRANCH, WIDTH, tn), lambda i, j: (layer, 0, 0, j)),
        ],
        out_specs=pl.BlockSpec((tm, tn), lambda i, j: (i, j)),
        out_shape=jax.ShapeDtypeStruct((s, d), BF16),
        compiler_params=_cparams("parallel", "parallel"),
        name="gate_merge",
    )(h, *ys, w_gate, w_gate, w_gate, w_gate, b_gate, w_branch)


def _out_proj_kernel(m_ref, x_ref, gn_ref, wo_ref, o_ref, hn_ref, *, tm):
    o_ref[...] = x_ref[...] + jnp.dot(m_ref[...], wo_ref[...], preferred_element_type=F32)
    _norm_into(o_ref, gn_ref, hn_ref, tm)


def _out_proj(merged, x, next_g, w_out, layer, norm_dtype, *, tm=512):
    s, d = x.shape
    row_spec = pl.BlockSpec((tm, d), lambda i: (i, 0))
    return pl.pallas_call(
        functools.partial(_out_proj_kernel, tm=tm),
        grid=(s // tm,),
        in_specs=[
            row_spec, row_spec,
            pl.BlockSpec((None, 1, d), lambda i: (layer, 0, 0)),
            pl.BlockSpec((None, d, d), lambda i: (layer, 0, 0)),
        ],
        out_specs=[row_spec, row_spec],
        out_shape=[jax.ShapeDtypeStruct((s, d), F32), jax.ShapeDtypeStruct((s, d), norm_dtype)],
        compiler_params=_cparams("parallel"),
        name="out_proj",
    )(merged, x, next_g, w_out)


def kernel(x, norm_g, w_in, conv_w, conv_b, conv_ln_g, conv_ln_b, sgu_ln_g, sgu_ln_b, sgu_w, sgu_b, w_branch, w_gate, b_gate, w_out, final_g):
    batch, s, d = x.shape
    assert batch == 1 and d == D_MODEL
    depth = norm_g.shape[0]
    w_out_b = w_out.astype(BF16)
    as_rows = lambda a: a.reshape(depth, 1, a.shape[-1])
    norm_g3, conv_b3 = as_rows(norm_g), as_rows(conv_b)
    conv_ln_g3, conv_ln_b3 = as_rows(conv_ln_g), as_rows(conv_ln_b)
    sgu_ln_g3, sgu_ln_b3 = as_rows(sgu_ln_g), as_rows(sgu_ln_b)
    b_gate3 = b_gate.reshape(depth, N_BRANCH, d)
    bs_full = jnp.repeat(jnp.swapaxes(sgu_b, 1, 2), WIDTH // SGU_GROUPS, axis=2)

    next_g3 = jnp.concatenate([norm_g3[1:], final_g.reshape(1, 1, d)], axis=0)

    xs = x.reshape(s, d)
    h = _first_norm(xs, norm_g3)
    for l in range(depth):
        p_mm, p_ew = _project(h, w_in, l)
        ya = _stick_breaking(p_mm, p_ew)
        yb = _conformer_conv(p_ew, conv_w, conv_b3, conv_ln_g3, conv_ln_b3, l)
        yc = _spatial_gating(p_ew, sgu_w, bs_full, sgu_ln_g3, sgu_ln_b3, l)
        yd = _dilated_mixture(p_ew)
        norm_dtype = F32 if l == depth - 1 else BF16
        merged = _gate_merge(h, (ya, yb, yc, yd), w_gate, b_gate3, w_branch, l)
        xs, h = _out_proj(merged, xs, next_g3, w_out_b, l, norm_dtype)
    return h.reshape(batch, s, d)
```

```python
import functools

import numpy as np
import jax
import jax.numpy as jnp
from jax import lax
from jax.experimental import pallas as pl
from jax.experimental.pallas import tpu as pltpu

F32 = jnp.float32
BF16 = jnp.bfloat16

D_MODEL = 2048
HEAD_DIM = 128
NORM_EPS = 1e-6
N_HEADS = 8
WIDTH = N_HEADS * HEAD_DIM
CONV_KERNEL = 31
CONV_HALO = 32
SGU_GROUPS = 8
SGU_CHUNK = 128
DIL_PATTERNS = ((128, 1), (512, 4), (2048, 16))
DIL_BLOCK = 128
N_BRANCH = 4
IN_WIDTH = 18432

MM_WIDTH = 3072
EW_WIDTH = IN_WIDTH - MM_WIDTH
MM_Q, MM_K, MM_V = 0, 1, 2
EW_AG, EW_BA, EW_BB, EW_BG, EW_CU, EW_CV, EW_CG, EW_DQ, EW_DK, EW_DV, EW_DG = 0, 1, 2, 3, 4, 5, 6, 7, 10, 13, 14

SB_DEAD = -104.0
NEG_BIG = -1e30
VMEM_LIMIT = 56 * 1024 * 1024


def _cparams(*sem):
    return pltpu.CompilerParams(dimension_semantics=sem, vmem_limit_bytes=VMEM_LIMIT)


def _sigmoid(x):
    return 1.0 / (1.0 + jnp.exp(-x))


def _silu(x):
    return x * _sigmoid(x)


def _gelu_tanh(x):
    c = np.sqrt(2.0 / np.pi).astype(np.float32)
    return x * (0.5 * (1.0 + jnp.tanh(c * (x + 0.044715 * (x * x * x)))))


def _rmsnorm_rows(x, g):
    ms = jnp.mean(x * x, axis=-1, keepdims=True)
    return x * lax.rsqrt(ms + NORM_EPS) * g


def _layernorm_rows(x, g, b):
    mu = jnp.mean(x, axis=-1, keepdims=True)
    xc = x - mu
    var = jnp.mean(xc * xc, axis=-1, keepdims=True)
    return xc * lax.rsqrt(var + NORM_EPS) * g + b


NORM_ROWS = 256


def _norm_into(x_ref, g_ref, h_ref, rows):
    g = g_ref[...]
    for c in range(rows // NORM_ROWS):
        sl = slice(c * NORM_ROWS, (c + 1) * NORM_ROWS)
        h_ref[sl, :] = _rmsnorm_rows(x_ref[sl, :], g).astype(h_ref.dtype)


def _first_norm_kernel(x_ref, g_ref, h_ref, *, ts):
    _norm_into(x_ref, g_ref, h_ref, ts)


def _first_norm(x, norm_g, *, ts=512):
    s, d = x.shape
    return pl.pallas_call(
        functools.partial(_first_norm_kernel, ts=ts),
        grid=(s // ts,),
        in_specs=[pl.BlockSpec((ts, d), lambda i: (i, 0)), pl.BlockSpec((None, 1, d), lambda i: (0, 0, 0))],
        out_specs=pl.BlockSpec((ts, d), lambda i: (i, 0)),
        out_shape=jax.ShapeDtypeStruct((s, d), BF16),
        compiler_params=_cparams("parallel"),
        name="first_norm",
    )(x, norm_g)


CAST_ROWS = 256


def _proj_kernel(h_ref, w_ref, o_ref, wb_ref):
    @pl.when(pl.program_id(1) == 0)
    def _():
        for c in range(w_ref.shape[0] // CAST_ROWS):
            sl = slice(c * CAST_ROWS, (c + 1) * CAST_ROWS)
            wb_ref[sl, :] = w_ref[sl, :].astype(BF16)

    o_ref[...] = jnp.dot(h_ref[...], wb_ref[...], preferred_element_type=F32).astype(o_ref.dtype)


def _project_cols(h, w_in, layer, col0, width, out_dtype, *, tm=1024, tn=1024):
    s, d = h.shape
    j0 = col0 // tn
    return pl.pallas_call(
        _proj_kernel,
        grid=(width // tn, s // tm),
        in_specs=[
            pl.BlockSpec((tm, d), lambda j, i: (i, 0)),
            pl.BlockSpec((None, d, tn), lambda j, i: (layer, 0, j0 + j)),
        ],
        out_specs=pl.BlockSpec((tm, tn), lambda j, i: (i, j)),
        out_shape=jax.ShapeDtypeStruct((s, width), out_dtype),
        scratch_shapes=[pltpu.VMEM((d, tn), BF16)],
        compiler_params=_cparams("arbitrary", "arbitrary"),
        name="in_proj",
    )(h, w_in)


def _project(h, w_in, layer):
    p_mm = _project_cols(h, w_in, layer, 0, MM_WIDTH, BF16)
    p_ew = _project_cols(h, w_in, layer, MM_WIDTH, EW_WIDTH, F32)
    return p_mm, p_ew


SB_BLOCK = 128
SB_ROWS = 256
SB_HEADS = 8


def _sb_kernel(uu_ref, q_ref, k_ref, v_ref, g_ref, o_ref, carry_ref, acc_ref):
    blk, tq = SB_BLOCK, SB_ROWS
    nsub = tq // blk
    i = pl.program_id(1)
    uu = uu_ref[...]
    scale = HEAD_DIM ** -0.5

    def step(j, r0, masked):
        start = pl.multiple_of(j * blk, blk)
        rows = slice(r0, tq)
        if masked:
            row = lax.broadcasted_iota(jnp.int32, (tq - r0, blk), 0)
            col = lax.broadcasted_iota(jnp.int32, (tq - r0, blk), 1)
            mask = col < row
        heads = range(SB_HEADS)
        cols = [slice(h * HEAD_DIM, (h + 1) * HEAD_DIM) for h in heads]
        carries = [carry_ref[h, rows, :] for h in heads]
        accs = [acc_ref[rows, cols[h]] for h in heads]
        nt = (((1,), (1,)), ((), ()))
        zs = [lax.dot_general(q_ref[rows, cols[h]], k_ref[pl.ds(start, blk), cols[h]], nt,
                              preferred_element_type=F32) * scale for h in heads]
        lbetas, splits = [], []
        for h in heads:
            sp = jnp.maximum(zs[h], 0.0) + jnp.log(1.0 + jnp.exp(-jnp.abs(zs[h])))
            l1mb = -sp
            if masked:
                l1mb = jnp.where(mask, l1mb, 0.0)
            hi = l1mb.astype(BF16)
            lo = (l1mb - hi.astype(F32)).astype(BF16)
            lbetas.append(zs[h] - sp)
            splits.append(jnp.concatenate([hi, lo], axis=1))
        rs = [jnp.dot(splits[h], uu, preferred_element_type=F32) for h in heads]
        ps = []
        for h in heads:
            p = jnp.exp(lbetas[h] + (carries[h] + rs[h][:, :blk]))
            if masked:
                p = jnp.where(mask, p, 0.0)
            ps.append(p.astype(BF16))
            carries[h] = carries[h] + rs[h][:, blk:]
        for h in heads:
            accs[h] = accs[h] + jnp.dot(ps[h], v_ref[pl.ds(start, blk), cols[h]], preferred_element_type=F32)
        for h in heads:
            carry_ref[h, rows, :] = carries[h]
            acc_ref[rows, cols[h]] = accs[h]
        top = functools.reduce(jnp.maximum, carries)
        return jnp.max(top)

    carry_ref[...] = jnp.zeros_like(carry_ref)
    acc_ref[...] = jnp.zeros_like(acc_ref)
    for c in reversed(range(nsub)):
        alive = step(i * nsub + c, c * blk, True)

    def cond(state):
        j, top = state
        return jnp.logical_and(j >= 0, top > SB_DEAD)

    def body(state):
        j, _ = state
        return j - 1, step(j, 0, False)

    lax.while_loop(cond, body, (i * nsub - 1, alive))
    o_ref[...] = (acc_ref[...] * _silu(g_ref[...])).astype(o_ref.dtype)


def _suffix_matrix(blk):
    j = np.arange(blk)[:, None]
    s = np.arange(blk)[None, :]
    half = np.concatenate([(j > s).astype(np.float32), np.ones((blk, blk), np.float32)], axis=1)
    return jnp.asarray(np.concatenate([half, half], axis=0), dtype=BF16)


def _stick_breaking(p_mm, p_ew):
    s = p_mm.shape[0]
    blk, tq, hw = SB_BLOCK, SB_ROWS, SB_HEADS * HEAD_DIM
    hb = WIDTH // hw
    return pl.pallas_call(
        _sb_kernel,
        grid=(hb, s // tq),
        in_specs=[
            pl.BlockSpec((2 * blk, 2 * blk), lambda h, i: (0, 0)),
            pl.BlockSpec((tq, hw), lambda h, i: (i, MM_Q * hb + h)),
            pl.BlockSpec((s, hw), lambda h, i: (0, MM_K * hb + h), pipeline_mode=pl.Buffered(1)),
            pl.BlockSpec((s, hw), lambda h, i: (0, MM_V * hb + h), pipeline_mode=pl.Buffered(1)),
            pl.BlockSpec((tq, hw), lambda h, i: (i, EW_AG * hb + h)),
        ],
        out_specs=pl.BlockSpec((tq, hw), lambda h, i: (i, h)),
        out_shape=jax.ShapeDtypeStruct((s, WIDTH), BF16),
        scratch_shapes=[pltpu.VMEM((SB_HEADS, tq, blk), F32), pltpu.VMEM((tq, hw), F32)],
        compiler_params=_cparams("parallel", "arbitrary"),
        name="sb_attn",
    )(_suffix_matrix(blk), p_mm, p_mm, p_mm, p_ew)


CONV_ROWS = 16
CONV_STAGE_ROWS = 64
CONV_SHIFT_ROWS = 40
SUBLANES = 8


def _conv_kernel(a_ref, b_ref, pa_ref, pb_ref, g_ref, w_ref, cb_ref, lng_ref, lnb_ref, o_ref, gs_ref, wb_ref, *, ts):
    i = pl.program_id(0)
    sub = SUBLANES
    halo = pa_ref[...] * _sigmoid(pb_ref[...])
    gs_ref[0, 0:CONV_HALO, :] = jnp.where(i > 0, halo, 0.0)
    for c in range(ts // CONV_STAGE_ROWS):
        rows = slice(c * CONV_STAGE_ROWS, (c + 1) * CONV_STAGE_ROWS)
        gs_ref[0, CONV_HALO + c * CONV_STAGE_ROWS:CONV_HALO + (c + 1) * CONV_STAGE_ROWS, :] = (
            a_ref[rows, :] * _sigmoid(b_ref[rows, :]))
    span = CONV_HALO + ts - sub
    for s in range(1, sub):
        for c in range(span // CONV_SHIFT_ROWS):
            r = c * CONV_SHIFT_ROWS
            gs_ref[s, r:r + CONV_SHIFT_ROWS, :] = gs_ref[0, r + s:r + s + CONV_SHIFT_ROWS, :]
    for k in range(CONV_KERNEL):
        wb_ref[k] = jnp.broadcast_to(w_ref[k:k + 1, :], (CONV_ROWS, WIDTH))
    first = CONV_HALO - (CONV_KERNEL - 1)
    bias = cb_ref[...]
    lng = lng_ref[...]
    lnb = lnb_ref[...]
    for c in range(ts // CONV_ROWS):
        base = c * CONV_ROWS
        acc = None
        for k in range(CONV_KERNEL):
            phase = (first + k) % sub
            r = base + first + k - phase
            term = gs_ref[phase, r:r + CONV_ROWS, :] * wb_ref[k]
            acc = term if acc is None else acc + term
        y = _silu(_layernorm_rows(acc + bias, lng, lnb))
        out = y * _silu(g_ref[base:base + CONV_ROWS, :])
        o_ref[base:base + CONV_ROWS, :] = out.astype(o_ref.dtype)


def _conformer_conv(p_ew, conv_w, conv_b, ln_g, ln_b, layer, *, ts=256):
    s = p_ew.shape[0]
    hb = ts // CONV_HALO
    row = lambda i: (i, 0)
    par = lambda i: (layer, 0, 0)
    return pl.pallas_call(
        functools.partial(_conv_kernel, ts=ts),
        grid=(s // ts,),
        in_specs=[
            pl.BlockSpec((ts, WIDTH), lambda i: (i, EW_BA)),
            pl.BlockSpec((ts, WIDTH), lambda i: (i, EW_BB)),
            pl.BlockSpec((CONV_HALO, WIDTH), lambda i: (jnp.maximum(i * hb - 1, 0), EW_BA)),
            pl.BlockSpec((CONV_HALO, WIDTH), lambda i: (jnp.maximum(i * hb - 1, 0), EW_BB)),
            pl.BlockSpec((ts, WIDTH), lambda i: (i, EW_BG)),
            pl.BlockSpec((None, CONV_KERNEL, WIDTH), par),
            pl.BlockSpec((None, 1, WIDTH), par),
            pl.BlockSpec((None, 1, WIDTH), par),
            pl.BlockSpec((None, 1, WIDTH), par),
        ],
        out_specs=pl.BlockSpec((ts, WIDTH), row),
        out_shape=jax.ShapeDtypeStruct((s, WIDTH), BF16),
        scratch_shapes=[pltpu.VMEM((SUBLANES, CONV_HALO + ts, WIDTH), F32),
                        pltpu.VMEM((CONV_KERNEL, CONV_ROWS, WIDTH), F32)],
        compiler_params=_cparams("parallel"),
        name="conformer_conv",
    )(p_ew, p_ew, p_ew, p_ew, p_ew, conv_w, conv_b, ln_g, ln_b)


def _sgu_kernel(u_ref, v_ref, g_ref, w_ref, bs_ref, lng_ref, lnb_ref, o_ref, *, ts):
    ck = SGU_CHUNK
    gw = WIDTH // SGU_GROUPS
    row = lax.broadcasted_iota(jnp.int32, (ck, ck), 0)
    col = lax.broadcasted_iota(jnp.int32, (ck, ck), 1)
    tril = col <= row
    lng = lng_ref[...]
    lnb = lnb_ref[...]
    ws = [jnp.where(tril, w_ref[g], 0.0).astype(BF16) for g in range(SGU_GROUPS)]
    for c in range(ts // ck):
        rows = slice(c * ck, (c + 1) * ck)
        vn = _layernorm_rows(_gelu_tanh(v_ref[rows, :]), lng, lnb).astype(BF16)
        for g in range(SGU_GROUPS):
            cols = slice(g * gw, (g + 1) * gw)
            z = jnp.dot(ws[g], vn[:, cols], preferred_element_type=F32) + bs_ref[:, cols]
            out = _gelu_tanh(u_ref[rows, cols]) * z * _silu(g_ref[rows, cols])
            o_ref[rows, cols] = out.astype(o_ref.dtype)


def _spatial_gating(p_ew, sgu_w, bs_full, ln_g, ln_b, layer, *, ts=256):
    s = p_ew.shape[0]
    par = lambda i: (layer, 0, 0)
    return pl.pallas_call(
        functools.partial(_sgu_kernel, ts=ts),
        grid=(s // ts,),
        in_specs=[
            pl.BlockSpec((ts, WIDTH), lambda i: (i, EW_CU)),
            pl.BlockSpec((ts, WIDTH), lambda i: (i, EW_CV)),
            pl.BlockSpec((ts, WIDTH), lambda i: (i, EW_CG)),
            pl.BlockSpec((None, SGU_GROUPS, SGU_CHUNK, SGU_CHUNK), lambda i: (layer, 0, 0, 0)),
            pl.BlockSpec((None, SGU_CHUNK, WIDTH), par),
            pl.BlockSpec((None, 1, WIDTH), par),
            pl.BlockSpec((None, 1, WIDTH), par),
        ],
        out_specs=pl.BlockSpec((ts, WIDTH), lambda i: (i, 0)),
        out_shape=jax.ShapeDtypeStruct((s, WIDTH), BF16),
        compiler_params=_cparams("parallel"),
        name="spatial_gating",
    )(p_ew, p_ew, p_ew, sgu_w, bs_full, ln_g, ln_b)


DIL_SUPER = 2048
DIL_COMBINE_ROWS = 256
DIL_UNROLL = 8


DIL_FREE_STRIDE = 4


def _class_reader(src_ref, start, count, dil, tmp_ref):
    if dil == 1:
        return lambda r: src_ref[start:start + count, :]
    if dil <= DIL_FREE_STRIDE:
        return lambda r: src_ref[pl.ds(start + r, count, stride=dil), :]
    f = DIL_FREE_STRIDE
    inner = dil // f
    assert inner <= f and dil % f == 0
    q = count * inner
    for a in range(f):
        tmp_ref[a * q:(a + 1) * q, :] = src_ref[pl.ds(start + a, q, stride=f), :]
    return lambda r: tmp_ref[pl.ds((r % f) * q + r // f, count, stride=inner), :]


def _classes_to_positions(src_ref, g, pos_ref, slot, dil, rows, tmp_ref):
    n = rows // dil
    if dil <= DIL_FREE_STRIDE:
        for r in range(dil):
            pos_ref[slot, pl.ds(r, n, stride=dil), :] = src_ref[g, r * n:(r + 1) * n, :]
        return
    f = DIL_FREE_STRIDE
    inner = dil // f
    q = n * inner
    for r in range(dil):
        tmp_ref[pl.ds((r % f) * q + r // f, n, stride=inner), :] = src_ref[g, r * n:(r + 1) * n, :]
    for a in range(f):
        pos_ref[slot, pl.ds(a, q, stride=f), :] = tmp_ref[a * q:(a + 1) * q, :]


def _dil_kernel(q0_ref, q1_ref, q2_ref, k0_ref, k1_ref, k2_ref, kp0_ref, kp1_ref, kp2_ref, v_ref, vp_ref, g_ref,
                o_ref, qs_ref, ks_ref, vs_ref, out_ref, lse_ref, pos_ref, tmp_ref):
    t, sup = DIL_BLOCK, DIL_SUPER
    i = pl.program_id(0)
    q_refs = (q0_ref, q1_ref, q2_ref)
    k_refs = (k0_ref, k1_ref, k2_ref)
    kp_refs = (kp0_ref, kp1_ref, kp2_ref)
    scale = HEAD_DIM ** -0.5
    row = lax.broadcasted_iota(jnp.int32, (t, 2 * t), 0)
    col = lax.broadcasted_iota(jnp.int32, (t, 2 * t), 1)
    band = jnp.logical_and(col >= row, col <= row + t)
    own = col >= t

    for g, (_, dil) in enumerate(DIL_PATTERNS):
        n = sup // dil
        sources = (
            (q_refs[g], 0, n, lambda r: (qs_ref, r * n)),
            (kp_refs[g], 0, t, lambda r: (ks_ref, r * (n + t))),
            (k_refs[g], 0, n, lambda r: (ks_ref, r * (n + t) + t)),
            (vp_ref, sup - t * dil, t, lambda r: (vs_ref, r * (n + t))),
            (v_ref, 0, n, lambda r: (vs_ref, r * (n + t) + t)),
        )
        for src_ref, start, count, dest in sources:
            read = _class_reader(src_ref, start, count, dil, tmp_ref)
            for r in range(dil):
                dst_ref, off = dest(r)
                dst_ref[g, off:off + count, :] = read(r).astype(BF16)

        per_class = n // t

        def blocks(c, carry, g=g, per_class=per_class):
            us = [c * DIL_UNROLL + b for b in range(DIL_UNROLL)]
            rs = [u // per_class for u in us]
            qoffs = [pl.multiple_of(u * t, t) for u in us]
            koffs = [pl.multiple_of((u + r) * t, t) for u, r in zip(us, rs)]
            nt = (((1,), (1,)), ((), ()))
            ss = [lax.dot_general(qs_ref[g, pl.ds(qo, t), :], ks_ref[g, pl.ds(ko, 2 * t), :], nt,
                                  preferred_element_type=F32) * scale for qo, ko in zip(qoffs, koffs)]
            ps, stats = [], []
            for u, r, s in zip(us, rs, ss):
                has_prev = jnp.logical_or(i > 0, u > r * per_class)
                s = jnp.where(jnp.logical_and(band, jnp.logical_or(own, has_prev)), s, NEG_BIG)
                m = jnp.max(s, axis=-1, keepdims=True)
                p = jnp.exp(s - m)
                ps.append(p.astype(BF16))
                stats.append((m, jnp.sum(p, axis=-1, keepdims=True)))
            nums = [jnp.dot(p, vs_ref[g, pl.ds(ko, 2 * t), :], preferred_element_type=F32) for p, ko in zip(ps, koffs)]
            for qo, num, (m, den) in zip(qoffs, nums, stats):
                out_ref[g, pl.ds(qo, t), :] = num * (1.0 / den)
                lse_ref[g, pl.ds(qo, t), :] = jnp.broadcast_to(m + jnp.log(den), (t, HEAD_DIM))
            return carry

        lax.fori_loop(0, sup // (t * DIL_UNROLL), blocks, 0)

    slot = 0
    for g in (1, 2):
        for src_ref in (out_ref, lse_ref):
            _classes_to_positions(src_ref, g, pos_ref, slot, DIL_PATTERNS[g][1], sup, tmp_ref)
            slot += 1

    cr = DIL_COMBINE_ROWS
    for c in range(sup // cr):
        rows = slice(c * cr, (c + 1) * cr)
        l0, l1, l2 = lse_ref[0, rows, :], pos_ref[1, rows, :], pos_ref[3, rows, :]
        m = jnp.maximum(jnp.maximum(l0, l1), l2)
        e0, e1, e2 = jnp.exp(l0 - m), jnp.exp(l1 - m), jnp.exp(l2 - m)
        mix = e0 * out_ref[0, rows, :] + e1 * pos_ref[0, rows, :] + e2 * pos_ref[2, rows, :]
        o_ref[rows, :] = ((mix / (e0 + e1 + e2)) * _silu(g_ref[rows, :])).astype(o_ref.dtype)


def _dilated_mixture(p_ew):
    s = p_ew.shape[0]
    t, sup = DIL_BLOCK, DIL_SUPER
    assert all(w // d == t for w, d in DIL_PATTERNS) and sup == t * DIL_PATTERNS[-1][1]
    nh = N_HEADS
    cur = lambda blk: (lambda i, h: (i, blk * nh + h))
    specs = [pl.BlockSpec((sup, HEAD_DIM), cur(EW_DQ + g)) for g in range(3)]
    specs += [pl.BlockSpec((sup, HEAD_DIM), cur(EW_DK + g)) for g in range(3)]
    for g, (_, dil) in enumerate(DIL_PATTERNS):
        rows = t * dil
        per = sup // rows
        specs.append(pl.BlockSpec((rows, HEAD_DIM),
                                  lambda i, h, g=g, per=per: (jnp.maximum(i * per - 1, 0), (EW_DK + g) * nh + h)))
    specs.append(pl.BlockSpec((sup, HEAD_DIM), cur(EW_DV)))
    specs.append(pl.BlockSpec((sup, HEAD_DIM), lambda i, h: (jnp.maximum(i - 1, 0), EW_DV * nh + h)))
    specs.append(pl.BlockSpec((sup, HEAD_DIM), cur(EW_DG)))
    hist = max(sup + t * d for _, d in DIL_PATTERNS)
    return pl.pallas_call(
        _dil_kernel,
        grid=(s // sup, nh),
        in_specs=specs,
        out_specs=pl.BlockSpec((sup, HEAD_DIM), lambda i, h: (i, h)),
        out_shape=jax.ShapeDtypeStruct((s, WIDTH), BF16),
        scratch_shapes=[
            pltpu.VMEM((3, sup, HEAD_DIM), BF16),
            pltpu.VMEM((3, hist, HEAD_DIM), BF16),
            pltpu.VMEM((3, hist, HEAD_DIM), BF16),
            pltpu.VMEM((3, sup, HEAD_DIM), F32),
            pltpu.VMEM((3, sup, HEAD_DIM), F32),
            pltpu.VMEM((4, sup, HEAD_DIM), F32),
            pltpu.VMEM((sup, HEAD_DIM), F32),
        ],
        compiler_params=_cparams("parallel", "parallel"),
        name="dilated_attn",
    )(*([p_ew] * 12))


def _gate_merge_kernel(h_ref, ya_ref, yb_ref, yc_ref, yd_ref, wg0_ref, wg1_ref, wg2_ref, wg3_ref,
                       bg_ref, wb_ref, o_ref):
    h = h_ref[...]
    merged = None
    branches = ((ya_ref, wg0_ref), (yb_ref, wg1_ref), (yc_ref, wg2_ref), (yd_ref, wg3_ref))
    for n, (y_ref, wg_ref) in enumerate(branches):
        gate = _sigmoid(jnp.dot(h, wg_ref[...].astype(BF16), preferred_element_type=F32) + bg_ref[n:n + 1, :])
        term = gate * jnp.dot(y_ref[...], wb_ref[n].astype(BF16), preferred_element_type=F32)
        merged = term if merged is None else merged + term
    o_ref[...] = merged.astype(o_ref.dtype)


def _gate_merge(h, ys, w_gate, b_gate, w_branch, layer, *, tm=1024, tn=256):
    s, d = h.shape
    nj = d // tn
    y_spec = pl.BlockSpec((tm, WIDTH), lambda i, j: (i, 0))
    wg_spec = lambda n: pl.BlockSpec((None, d, tn), lambda i, j: (layer, 0, n * nj + j))
    return pl.pallas_call(
        _gate_merge_kernel,
        grid=(s // tm, nj),
        in_specs=[
            pl.BlockSpec((tm, d), lambda i, j: (i, 0)),
            y_spec, y_spec, y_spec, y_spec,
            wg_spec(0), wg_spec(1), wg_spec(2), wg_spec(3),
            pl.BlockSpec((None, N_BRANCH, tn), lambda i, j: (layer, 0, j)),
            pl.BlockSpec((None, N_BRANCH, WIDTH, tn), lambda i, j: (layer, 0, 0, j)),
        ],
        out_specs=pl.BlockSpec((tm, tn), lambda i, j: (i, j)),
        out_shape=jax.ShapeDtypeStruct((s, d), BF16),
        compiler_params=_cparams("parallel", "parallel"),
        name="gate_merge",
    )(h, *ys, w_gate, w_gate, w_gate, w_gate, b_gate, w_branch)


def _out_proj_kernel(m_ref, x_ref, gn_ref, wo_ref, o_ref, hn_ref, *, tm):
    o_ref[...] = x_ref[...] + jnp.dot(m_ref[...], wo_ref[...], preferred_element_type=F32)
    _norm_into(o_ref, gn_ref, hn_ref, tm)


def _out_proj(merged, x, next_g, w_out, layer, norm_dtype, *, tm=512):
    s, d = x.shape
    row_spec = pl.BlockSpec((tm, d), lambda i: (i, 0))
    return pl.pallas_call(
        functools.partial(_out_proj_kernel, tm=tm),
        grid=(s // tm,),
        in_specs=[
            row_spec, row_spec,
            pl.BlockSpec((None, 1, d), lambda i: (layer, 0, 0)),
            pl.BlockSpec((None, d, d), lambda i: (layer, 0, 0)),
        ],
        out_specs=[row_spec, row_spec],
        out_shape=[jax.ShapeDtypeStruct((s, d), F32), jax.ShapeDtypeStruct((s, d), norm_dtype)],
        compiler_params=_cparams("parallel"),
        name="out_proj",
    )(merged, x, next_g, w_out)


def kernel(x, norm_g, w_in, conv_w, conv_b, conv_ln_g, conv_ln_b, sgu_ln_g, sgu_ln_b, sgu_w, sgu_b, w_branch, w_gate, b_gate, w_out, final_g):
    batch, s, d = x.shape
    assert batch == 1 and d == D_MODEL
    depth = norm_g.shape[0]
    w_out_b = w_out.astype(BF16)
    as_rows = lambda a: a.reshape(depth, 1, a.shape[-1])
    norm_g3, conv_b3 = as_rows(norm_g), as_rows(conv_b)
    conv_ln_g3, conv_ln_b3 = as_rows(conv_ln_g), as_rows(conv_ln_b)
    sgu_ln_g3, sgu_ln_b3 = as_rows(sgu_ln_g), as_rows(sgu_ln_b)
    b_gate3 = b_gate.reshape(depth, N_BRANCH, d)
    bs_full = jnp.repeat(jnp.swapaxes(sgu_b, 1, 2), WIDTH // SGU_GROUPS, axis=2)

    next_g3 = jnp.concatenate([norm_g3[1:], final_g.reshape(1, 1, d)], axis=0)

    xs = x.reshape(s, d)
    h = _first_norm(xs, norm_g3)
    for l in range(depth):
        p_mm, p_ew = _project(h, w_in, l)
        ya = _stick_breaking(p_mm, p_ew)
        yb = _conformer_conv(p_ew, conv_w, conv_b3, conv_ln_g3, conv_ln_b3, l)
        yc = _spatial_gating(p_ew, sgu_w, bs_full, sgu_ln_g3, sgu_ln_b3, l)
        yd = _dilated_mixture(p_ew)
        norm_dtype = F32 if l == depth - 1 else BF16
        merged = _gate_merge(h, (ya, yb, yc, yd), w_gate, b_gate3, w_branch, l)
        xs, h = _out_proj(merged, xs, next_g3, w_out_b, l, norm_dtype)
    return h.reshape(batch, s, d)
```

```python
import functools

import numpy as np
import jax
import jax.numpy as jnp
from jax import lax
from jax.experimental import pallas as pl
from jax.experimental.pallas import tpu as pltpu

F32 = jnp.float32
BF16 = jnp.bfloat16

D_MODEL = 2048
HEAD_DIM = 128
NORM_EPS = 1e-6
N_HEADS = 8
WIDTH = N_HEADS * HEAD_DIM
CONV_KERNEL = 31
CONV_HALO = 32
SGU_GROUPS = 8
SGU_CHUNK = 128
DIL_PATTERNS = ((128, 1), (512, 4), (2048, 16))
DIL_BLOCK = 128
N_BRANCH = 4
IN_WIDTH = 18432

MM_WIDTH = 3072
EW_WIDTH = IN_WIDTH - MM_WIDTH
MM_Q, MM_K, MM_V = 0, 1, 2
EW_AG, EW_BA, EW_BB, EW_BG, EW_CU, EW_CV, EW_CG, EW_DQ, EW_DK, EW_DV, EW_DG = 0, 1, 2, 3, 4, 5, 6, 7, 10, 13, 14

SB_DEAD = -104.0
NEG_BIG = -1e30
VMEM_LIMIT = 56 * 1024 * 1024


def _cparams(*sem):
    return pltpu.CompilerParams(dimension_semantics=sem, vmem_limit_bytes=VMEM_LIMIT)


def _sigmoid(x):
    return 1.0 / (1.0 + jnp.exp(-x))


def _silu(x):
    return x * _sigmoid(x)


def _gelu_tanh(x):
    c = np.sqrt(2.0 / np.pi).astype(np.float32)
    return x * (0.5 * (1.0 + jnp.tanh(c * (x + 0.044715 * (x * x * x)))))


def _rmsnorm_rows(x, g):
    ms = jnp.mean(x * x, axis=-1, keepdims=True)
    return x * lax.rsqrt(ms + NORM_EPS) * g


def _layernorm_rows(x, g, b):
    mu = jnp.mean(x, axis=-1, keepdims=True)
    xc = x - mu
    var = jnp.mean(xc * xc, axis=-1, keepdims=True)
    return xc * lax.rsqrt(var + NORM_EPS) * g + b


NORM_ROWS = 256


def _norm_into(x_ref, g_ref, h_ref, rows):
    g = g_ref[...]
    for c in range(rows // NORM_ROWS):
        sl = slice(c * NORM_ROWS, (c + 1) * NORM_ROWS)
        h_ref[sl, :] = _rmsnorm_rows(x_ref[sl, :], g).astype(h_ref.dtype)


def _first_norm_kernel(x_ref, g_ref, h_ref, *, ts):
    _norm_into(x_ref, g_ref, h_ref, ts)


def _first_norm(x, norm_g, *, ts=512):
    s, d = x.shape
    return pl.pallas_call(
        functools.partial(_first_norm_kernel, ts=ts),
        grid=(s // ts,),
        in_specs=[pl.BlockSpec((ts, d), lambda i: (i, 0)), pl.BlockSpec((None, 1, d), lambda i: (0, 0, 0))],
        out_specs=pl.BlockSpec((ts, d), lambda i: (i, 0)),
        out_shape=jax.ShapeDtypeStruct((s, d), BF16),
        compiler_params=_cparams("parallel"),
        name="first_norm",
    )(x, norm_g)


CAST_ROWS = 256


def _proj_kernel(h_ref, w_ref, o_ref, wb_ref):
    @pl.when(pl.program_id(1) == 0)
    def _():
        for c in range(w_ref.shape[0] // CAST_ROWS):
            sl = slice(c * CAST_ROWS, (c + 1) * CAST_ROWS)
            wb_ref[sl, :] = w_ref[sl, :].astype(BF16)

    o_ref[...] = jnp.dot(h_ref[...], wb_ref[...], preferred_element_type=F32).astype(o_ref.dtype)


def _project_cols(h, w_in, layer, col0, width, out_dtype, *, tm=1024, tn=1024):
    s, d = h.shape
    j0 = col0 // tn
    return pl.pallas_call(
        _proj_kernel,
        grid=(width // tn, s // tm),
        in_specs=[
            pl.BlockSpec((tm, d), lambda j, i: (i, 0)),
            pl.BlockSpec((None, d, tn), lambda j, i: (layer, 0, j0 + j)),
        ],
        out_specs=pl.BlockSpec((tm, tn), lambda j, i: (i, j)),
        out_shape=jax.ShapeDtypeStruct((s, width), out_dtype),
        scratch_shapes=[pltpu.VMEM((d, tn), BF16)],
        compiler_params=_cparams("arbitrary", "arbitrary"),
        name="in_proj",
    )(h, w_in)


def _project(h, w_in, layer):
    p_mm = _project_cols(h, w_in, layer, 0, MM_WIDTH, BF16)
    p_ew = _project_cols(h, w_in, layer, MM_WIDTH, EW_WIDTH, F32)
    return p_mm, p_ew


SB_BLOCK = 128
SB_ROWS = 256
SB_HEADS = 8


def _sb_kernel(uu_ref, q_ref, k_ref, v_ref, g_ref, o_ref, carry_ref, acc_ref):
    blk, tq = SB_BLOCK, SB_ROWS
    nsub = tq // blk
    i = pl.program_id(1)
    uu = uu_ref[...]
    scale = HEAD_DIM ** -0.5

    def step(j, r0, masked):
        start = pl.multiple_of(j * blk, blk)
        rows = slice(r0, tq)
        if masked:
            row = lax.broadcasted_iota(jnp.int32, (tq - r0, blk), 0)
            col = lax.broadcasted_iota(jnp.int32, (tq - r0, blk), 1)
            mask = col < row
        heads = range(SB_HEADS)
        cols = [slice(h * HEAD_DIM, (h + 1) * HEAD_DIM) for h in heads]
        carries = [carry_ref[h, rows, :] for h in heads]
        accs = [acc_ref[rows, cols[h]] for h in heads]
        nt = (((1,), (1,)), ((), ()))
        zs = [lax.dot_general(q_ref[rows, cols[h]], k_ref[pl.ds(start, blk), cols[h]], nt,
                              preferred_element_type=F32) * scale for h in heads]
        lbetas, splits = [], []
        for h in heads:
            sp = jnp.maximum(zs[h], 0.0) + jnp.log(1.0 + jnp.exp(-jnp.abs(zs[h])))
            l1mb = -sp
            if masked:
                l1mb = jnp.where(mask, l1mb, 0.0)
            hi = l1mb.astype(BF16)
            lo = (l1mb - hi.astype(F32)).astype(BF16)
            lbetas.append(zs[h] - sp)
            splits.append(jnp.concatenate([hi, lo], axis=1))
        rs = [jnp.dot(splits[h], uu, preferred_element_type=F32) for h in heads]
        ps = []
        for h in heads:
            p = jnp.exp(lbetas[h] + (carries[h] + rs[h][:, :blk]))
            if masked:
                p = jnp.where(mask, p, 0.0)
            ps.append(p.astype(BF16))
            carries[h] = carries[h] + rs[h][:, blk:]
        for h in heads:
            accs[h] = accs[h] + jnp.dot(ps[h], v_ref[pl.ds(start, blk), cols[h]], preferred_element_type=F32)
        for h in heads:
            carry_ref[h, rows, :] = carries[h]
            acc_ref[rows, cols[h]] = accs[h]
        top = functools.reduce(jnp.maximum, carries)
        return jnp.max(top)

    carry_ref[...] = jnp.zeros_like(carry_ref)
    acc_ref[...] = jnp.zeros_like(acc_ref)
    for c in reversed(range(nsub)):
        alive = step(i * nsub + c, c * blk, True)

    def cond(state):
        j, top = state
        return jnp.logical_and(j >= 0, top > SB_DEAD)

    def body(state):
        j, _ = state
        return j - 1, step(j, 0, False)

    lax.while_loop(cond, body, (i * nsub - 1, alive))
    o_ref[...] = (acc_ref[...] * _silu(g_ref[...])).astype(o_ref.dtype)


def _suffix_matrix(blk):
    j = np.arange(blk)[:, None]
    s = np.arange(blk)[None, :]
    half = np.concatenate([(j > s).astype(np.float32), np.ones((blk, blk), np.float32)], axis=1)
    return jnp.asarray(np.concatenate([half, half], axis=0), dtype=BF16)


def _stick_breaking(p_mm, p_ew):
    s = p_mm.shape[0]
    blk, tq, hw = SB_BLOCK, SB_ROWS, SB_HEADS * HEAD_DIM
    hb = WIDTH // hw
    return pl.pallas_call(
        _sb_kernel,
        grid=(hb, s // tq),
        in_specs=[
            pl.BlockSpec((2 * blk, 2 * blk), lambda h, i: (0, 0)),
            pl.BlockSpec((tq, hw), lambda h, i: (i, MM_Q * hb + h)),
            pl.BlockSpec((s, hw), lambda h, i: (0, MM_K * hb + h), pipeline_mode=pl.Buffered(1)),
            pl.BlockSpec((s, hw), lambda h, i: (0, MM_V * hb + h), pipeline_mode=pl.Buffered(1)),
            pl.BlockSpec((tq, hw), lambda h, i: (i, EW_AG * hb + h)),
        ],
        out_specs=pl.BlockSpec((tq, hw), lambda h, i: (i, h)),
        out_shape=jax.ShapeDtypeStruct((s, WIDTH), BF16),
        scratch_shapes=[pltpu.VMEM((SB_HEADS, tq, blk), F32), pltpu.VMEM((tq, hw), F32)],
        compiler_params=_cparams("parallel", "arbitrary"),
        name="sb_attn",
    )(_suffix_matrix(blk), p_mm, p_mm, p_mm, p_ew)


CONV_ROWS = 16
CONV_STAGE_ROWS = 64
CONV_SHIFT_ROWS = 40
SUBLANES = 8


def _conv_kernel(a_ref, b_ref, pa_ref, pb_ref, g_ref, w_ref, cb_ref, lng_ref, lnb_ref, o_ref, gs_ref, wb_ref, *, ts):
    i = pl.program_id(0)
    sub = SUBLANES
    halo = pa_ref[...] * _sigmoid(pb_ref[...])
    gs_ref[0, 0:CONV_HALO, :] = jnp.where(i > 0, halo, 0.0)
    for c in range(ts // CONV_STAGE_ROWS):
        rows = slice(c * CONV_STAGE_ROWS, (c + 1) * CONV_STAGE_ROWS)
        gs_ref[0, CONV_HALO + c * CONV_STAGE_ROWS:CONV_HALO + (c + 1) * CONV_STAGE_ROWS, :] = (
            a_ref[rows, :] * _sigmoid(b_ref[rows, :]))
    span = CONV_HALO + ts - sub
    for s in range(1, sub):
        for c in range(span // CONV_SHIFT_ROWS):
            r = c * CONV_SHIFT_ROWS
            gs_ref[s, r:r + CONV_SHIFT_ROWS, :] = gs_ref[0, r + s:r + s + CONV_SHIFT_ROWS, :]
    for k in range(CONV_KERNEL):
        wb_ref[k] = jnp.broadcast_to(w_ref[k:k + 1, :], (CONV_ROWS, WIDTH))
    first = CONV_HALO - (CONV_KERNEL - 1)
    bias = cb_ref[...]
    lng = lng_ref[...]
    lnb = lnb_ref[...]
    for c in range(ts // CONV_ROWS):
        base = c * CONV_ROWS
        acc = None
        for k in range(CONV_KERNEL):
            phase = (first + k) % sub
            r = base + first + k - phase
            term = gs_ref[phase, r:r + CONV_ROWS, :] * wb_ref[k]
            acc = term if acc is None else acc + term
        y = _silu(_layernorm_rows(acc + bias, lng, lnb))
        out = y * _silu(g_ref[base:base + CONV_ROWS, :])
        o_ref[base:base + CONV_ROWS, :] = out.astype(o_ref.dtype)


def _conformer_conv(p_ew, conv_w, conv_b, ln_g, ln_b, layer, *, ts=256):
    s = p_ew.shape[0]
    hb = ts // CONV_HALO
    row = lambda i: (i, 0)
    par = lambda i: (layer, 0, 0)
    return pl.pallas_call(
        functools.partial(_conv_kernel, ts=ts),
        grid=(s // ts,),
        in_specs=[
            pl.BlockSpec((ts, WIDTH), lambda i: (i, EW_BA)),
            pl.BlockSpec((ts, WIDTH), lambda i: (i, EW_BB)),
            pl.BlockSpec((CONV_HALO, WIDTH), lambda i: (jnp.maximum(i * hb - 1, 0), EW_BA)),
            pl.BlockSpec((CONV_HALO, WIDTH), lambda i: (jnp.maximum(i * hb - 1, 0), EW_BB)),
            pl.BlockSpec((ts, WIDTH), lambda i: (i, EW_BG)),
            pl.BlockSpec((None, CONV_KERNEL, WIDTH), par),
            pl.BlockSpec((None, 1, WIDTH), par),
            pl.BlockSpec((None, 1, WIDTH), par),
            pl.BlockSpec((None, 1, WIDTH), par),
        ],
        out_specs=pl.BlockSpec((ts, WIDTH), row),
        out_shape=jax.ShapeDtypeStruct((s, WIDTH), BF16),
        scratch_shapes=[pltpu.VMEM((SUBLANES, CONV_HALO + ts, WIDTH), F32),
                        pltpu.VMEM((CONV_KERNEL, CONV_ROWS, WIDTH), F32)],
        compiler_params=_cparams("parallel"),
        name="conformer_conv",
    )(p_ew, p_ew, p_ew, p_ew, p_ew, conv_w, conv_b, ln_g, ln_b)


def _sgu_kernel(u_ref, v_ref, g_ref, w_ref, bs_ref, lng_ref, lnb_ref, o_ref, *, ts):
    ck = SGU_CHUNK
    gw = WIDTH // SGU_GROUPS
    row = lax.broadcasted_iota(jnp.int32, (ck, ck), 0)
    col = lax.broadcasted_iota(jnp.int32, (ck, ck), 1)
    tril = col <= row
    lng = lng_ref[...]
    lnb = lnb_ref[...]
    ws = [jnp.where(tril, w_ref[g], 0.0).astype(BF16) for g in range(SGU_GROUPS)]
    for c in range(ts // ck):
        rows = slice(c * ck, (c + 1) * ck)
        vn = _layernorm_rows(_gelu_tanh(v_ref[rows, :]), lng, lnb).astype(BF16)
        for g in range(SGU_GROUPS):
            cols = slice(g * gw, (g + 1) * gw)
            z = jnp.dot(ws[g], vn[:, cols], preferred_element_type=F32) + bs_ref[:, cols]
            out = _gelu_tanh(u_ref[rows, cols]) * z * _silu(g_ref[rows, cols])
            o_ref[rows, cols] = out.astype(o_ref.dtype)


def _spatial_gating(p_ew, sgu_w, bs_full, ln_g, ln_b, layer, *, ts=256):
    s = p_ew.shape[0]
    par = lambda i: (layer, 0, 0)
    return pl.pallas_call(
        functools.partial(_sgu_kernel, ts=ts),
        grid=(s // ts,),
        in_specs=[
            pl.BlockSpec((ts, WIDTH), lambda i: (i, EW_CU)),
            pl.BlockSpec((ts, WIDTH), lambda i: (i, EW_CV)),
            pl.BlockSpec((ts, WIDTH), lambda i: (i, EW_CG)),
            pl.BlockSpec((None, SGU_GROUPS, SGU_CHUNK, SGU_CHUNK), lambda i: (layer, 0, 0, 0)),
            pl.BlockSpec((None, SGU_CHUNK, WIDTH), par),
            pl.BlockSpec((None, 1, WIDTH), par),
            pl.BlockSpec((None, 1, WIDTH), par),
        ],
        out_specs=pl.BlockSpec((ts, WIDTH), lambda i: (i, 0)),
        out_shape=jax.ShapeDtypeStruct((s, WIDTH), BF16),
        compiler_params=_cparams("parallel"),
        name="spatial_gating",
    )(p_ew, p_ew, p_ew, sgu_w, bs_full, ln_g, ln_b)


DIL_SUPER = 2048
DIL_COMBINE_ROWS = 256
DIL_UNROLL = 16


DIL_FREE_STRIDE = 4


def _class_reader(src_ref, start, count, dil, tmp_ref):
    if dil == 1:
        return lambda r: src_ref[start:start + count, :]
    if dil <= DIL_FREE_STRIDE:
        return lambda r: src_ref[pl.ds(start + r, count, stride=dil), :]
    f = DIL_FREE_STRIDE
    inner = dil // f
    assert inner <= f and dil % f == 0
    q = count * inner
    for a in range(f):
        tmp_ref[a * q:(a + 1) * q, :] = src_ref[pl.ds(start + a, q, stride=f), :]
    return lambda r: tmp_ref[pl.ds((r % f) * q + r // f, count, stride=inner), :]


def _classes_to_positions(src_ref, g, pos_ref, slot, dil, rows, tmp_ref):
    n = rows // dil
    if dil <= DIL_FREE_STRIDE:
        for r in range(dil):
            pos_ref[slot, pl.ds(r, n, stride=dil), :] = src_ref[g, r * n:(r + 1) * n, :]
        return
    f = DIL_FREE_STRIDE
    inner = dil // f
    q = n * inner
    for r in range(dil):
        tmp_ref[pl.ds((r % f) * q + r // f, n, stride=inner), :] = src_ref[g, r * n:(r + 1) * n, :]
    for a in range(f):
        pos_ref[slot, pl.ds(a, q, stride=f), :] = tmp_ref[a * q:(a + 1) * q, :]


def _dil_kernel(q0_ref, q1_ref, q2_ref, k0_ref, k1_ref, k2_ref, kp0_ref, kp1_ref, kp2_ref, v_ref, vp_ref, g_ref,
                o_ref, qs_ref, ks_ref, vs_ref, out_ref, lse_ref, pos_ref, tmp_ref):
    t, sup = DIL_BLOCK, DIL_SUPER
    i = pl.program_id(0)
    q_refs = (q0_ref, q1_ref, q2_ref)
    k_refs = (k0_ref, k1_ref, k2_ref)
    kp_refs = (kp0_ref, kp1_ref, kp2_ref)
    scale = HEAD_DIM ** -0.5
    row = lax.broadcasted_iota(jnp.int32, (t, 2 * t), 0)
    col = lax.broadcasted_iota(jnp.int32, (t, 2 * t), 1)
    band = jnp.logical_and(col >= row, col <= row + t)
    own = col >= t

    for g, (_, dil) in enumerate(DIL_PATTERNS):
        n = sup // dil
        sources = (
            (q_refs[g], 0, n, lambda r: (qs_ref, r * n)),
            (kp_refs[g], 0, t, lambda r: (ks_ref, r * (n + t))),
            (k_refs[g], 0, n, lambda r: (ks_ref, r * (n + t) + t)),
            (vp_ref, sup - t * dil, t, lambda r: (vs_ref, r * (n + t))),
            (v_ref, 0, n, lambda r: (vs_ref, r * (n + t) + t)),
        )
        for src_ref, start, count, dest in sources:
            read = _class_reader(src_ref, start, count, dil, tmp_ref)
            for r in range(dil):
                dst_ref, off = dest(r)
                dst_ref[g, off:off + count, :] = read(r).astype(BF16)

        per_class = n // t

        def blocks(c, carry, g=g, per_class=per_class):
            us = [c * DIL_UNROLL + b for b in range(DIL_UNROLL)]
            rs = [u // per_class for u in us]
            qoffs = [pl.multiple_of(u * t, t) for u in us]
            koffs = [pl.multiple_of((u + r) * t, t) for u, r in zip(us, rs)]
            nt = (((1,), (1,)), ((), ()))
            ss = [lax.dot_general(qs_ref[g, pl.ds(qo, t), :], ks_ref[g, pl.ds(ko, 2 * t), :], nt,
                                  preferred_element_type=F32) * scale for qo, ko in zip(qoffs, koffs)]
            ps, stats = [], []
            for u, r, s in zip(us, rs, ss):
                has_prev = jnp.logical_or(i > 0, u > r * per_class)
                s = jnp.where(jnp.logical_and(band, jnp.logical_or(own, has_prev)), s, NEG_BIG)
                m = jnp.max(s, axis=-1, keepdims=True)
                p = jnp.exp(s - m)
                ps.append(p.astype(BF16))
                stats.append((m, jnp.sum(p, axis=-1, keepdims=True)))
            nums = [jnp.dot(p, vs_ref[g, pl.ds(ko, 2 * t), :], preferred_element_type=F32) for p, ko in zip(ps, koffs)]
            for qo, num, (m, den) in zip(qoffs, nums, stats):
                out_ref[g, pl.ds(qo, t), :] = num * (1.0 / den)
                lse_ref[g, pl.ds(qo, t), :] = jnp.broadcast_to(m + jnp.log(den), (t, HEAD_DIM))
            return carry

        lax.fori_loop(0, sup // (t * DIL_UNROLL), blocks, 0)

    slot = 0
    for g in (1, 2):
        for src_ref in (out_ref, lse_ref):
            _classes_to_positions(src_ref, g, pos_ref, slot, DIL_PATTERNS[g][1], sup, tmp_ref)
            slot += 1

    cr = DIL_COMBINE_ROWS
    for c in range(sup // cr):
        rows = slice(c * cr, (c + 1) * cr)
        l0, l1, l2 = lse_ref[0, rows, :], pos_ref[1, rows, :], pos_ref[3, rows, :]
        m = jnp.maximum(jnp.maximum(l0, l1), l2)
        e0, e1, e2 = jnp.exp(l0 - m), jnp.exp(l1 - m), jnp.exp(l2 - m)
        mix = e0 * out_ref[0, rows, :] + e1 * pos_ref[0, rows, :] + e2 * pos_ref[2, rows, :]
        o_ref[rows, :] = ((mix / (e0 + e1 + e2)) * _silu(g_ref[rows, :])).astype(o_ref.dtype)


def _dilated_mixture(p_ew):
    s = p_ew.shape[0]
    t, sup = DIL_BLOCK, DIL_SUPER
    assert all(w // d == t for w, d in DIL_PATTERNS) and sup == t * DIL_PATTERNS[-1][1]
    nh = N_HEADS
    cur = lambda blk: (lambda i, h: (i, blk * nh + h))
    specs = [pl.BlockSpec((sup, HEAD_DIM), cur(EW_DQ + g)) for g in range(3)]
    specs += [pl.BlockSpec((sup, HEAD_DIM), cur(EW_DK + g)) for g in range(3)]
    for g, (_, dil) in enumerate(DIL_PATTERNS):
        rows = t * dil
        per = sup // rows
        specs.append(pl.BlockSpec((rows, HEAD_DIM),
                                  lambda i, h, g=g, per=per: (jnp.maximum(i * per - 1, 0), (EW_DK + g) * nh + h)))
    specs.append(pl.BlockSpec((sup, HEAD_DIM), cur(EW_DV)))
    specs.append(pl.BlockSpec((sup, HEAD_DIM), lambda i, h: (jnp.maximum(i - 1, 0), EW_DV * nh + h)))
    specs.append(pl.BlockSpec((sup, HEAD_DIM), cur(EW_DG)))
    hist = max(sup + t * d for _, d in DIL_PATTERNS)
    return pl.pallas_call(
        _dil_kernel,
        grid=(s // sup, nh),
        in_specs=specs,
        out_specs=pl.BlockSpec((sup, HEAD_DIM), lambda i, h: (i, h)),
        out_shape=jax.ShapeDtypeStruct((s, WIDTH), BF16),
        scratch_shapes=[
            pltpu.VMEM((3, sup, HEAD_DIM), BF16),
            pltpu.VMEM((3, hist, HEAD_DIM), BF16),
            pltpu.VMEM((3, hist, HEAD_DIM), BF16),
            pltpu.VMEM((3, sup, HEAD_DIM), F32),
            pltpu.VMEM((3, sup, HEAD_DIM), F32),
            pltpu.VMEM((4, sup, HEAD_DIM), F32),
            pltpu.VMEM((sup, HEAD_DIM), F32),
        ],
        compiler_params=_cparams("parallel", "parallel"),
        name="dilated_attn",
    )(*([p_ew] * 12))


def _gate_merge_kernel(h_ref, ya_ref, yb_ref, yc_ref, yd_ref, wg0_ref, wg1_ref, wg2_ref, wg3_ref,
                       bg_ref, wb_ref, o_ref):
    h = h_ref[...]
    merged = None
    branches = ((ya_ref, wg0_ref), (yb_ref, wg1_ref), (yc_ref, wg2_ref), (yd_ref, wg3_ref))
    for n, (y_ref, wg_ref) in enumerate(branches):
        gate = _sigmoid(jnp.dot(h, wg_ref[...].astype(BF16), preferred_element_type=F32) + bg_ref[n:n + 1, :])
        term = gate * jnp.dot(y_ref[...], wb_ref[n].astype(BF16), preferred_element_type=F32)
        merged = term if merged is None else merged + term
    o_ref[...] = merged.astype(o_ref.dtype)


def _gate_merge(h, ys, w_gate, b_gate, w_branch, layer, *, tm=1024, tn=256):
    s, d = h.shape
    nj = d // tn
    y_spec = pl.BlockSpec((tm, WIDTH), lambda i, j: (i, 0))
    wg_spec = lambda n: pl.BlockSpec((None, d, tn), lambda i, j: (layer, 0, n * nj + j))
    return pl.pallas_call(
        _gate_merge_kernel,
        grid=(s // tm, nj),
        in_specs=[
            pl.BlockSpec((tm, d), lambda i, j: (i, 0)),
            y_spec, y_spec, y_spec, y_spec,
            wg_spec(0), wg_spec(1), wg_spec(2), wg_spec(3),
            pl.BlockSpec((None, N_BRANCH, tn), lambda i, j: (layer, 0, j)),
            pl.BlockSpec((None, N_BRANCH, WIDTH, tn), lambda i, j: (layer, 0, 0, j)),
        ],
        out_specs=pl.BlockSpec((tm, tn), lambda i, j: (i, j)),
        out_shape=jax.ShapeDtypeStruct((s, d), BF16),
        compiler_params=_cparams("parallel", "parallel"),
        name="gate_merge",
    )(h, *ys, w_gate, w_gate, w_gate, w_gate, b_gate, w_branch)


def _out_proj_kernel(m_ref, x_ref, gn_ref, wo_ref, o_ref, hn_ref, *, tm):
    o_ref[...] = x_ref[...] + jnp.dot(m_ref[...], wo_ref[...], preferred_element_type=F32)
    _norm_into(o_ref, gn_ref, hn_ref, tm)


def _out_proj(merged, x, next_g, w_out, layer, norm_dtype, *, tm=512):
    s, d = x.shape
    row_spec = pl.BlockSpec((tm, d), lambda i: (i, 0))
    return pl.pallas_call(
        functools.partial(_out_proj_kernel, tm=tm),
        grid=(s // tm,),
        in_specs=[
            row_spec, row_spec,
            pl.BlockSpec((None, 1, d), lambda i: (layer, 0, 0)),
            pl.BlockSpec((None, d, d), lambda i: (layer, 0, 0)),
        ],
        out_specs=[row_spec, row_spec],
        out_shape=[jax.ShapeDtypeStruct((s, d), F32), jax.ShapeDtypeStruct((s, d), norm_dtype)],
        compiler_params=_cparams("parallel"),
        name="out_proj",
    )(merged, x, next_g, w_out)


def kernel(x, norm_g, w_in, conv_w, conv_b, conv_ln_g, conv_ln_b, sgu_ln_g, sgu_ln_b, sgu_w, sgu_b, w_branch, w_gate, b_gate, w_out, final_g):
    batch, s, d = x.shape
    assert batch == 1 and d == D_MODEL
    depth = norm_g.shape[0]
    w_out_b = w_out.astype(BF16)
    as_rows = lambda a: a.reshape(depth, 1, a.shape[-1])
    norm_g3, conv_b3 = as_rows(norm_g), as_rows(conv_b)
    conv_ln_g3, conv_ln_b3 = as_rows(conv_ln_g), as_rows(conv_ln_b)
    sgu_ln_g3, sgu_ln_b3 = as_rows(sgu_ln_g), as_rows(sgu_ln_b)
    b_gate3 = b_gate.reshape(depth, N_BRANCH, d)
    bs_full = jnp.repeat(jnp.swapaxes(sgu_b, 1, 2), WIDTH // SGU_GROUPS, axis=2)

    next_g3 = jnp.concatenate([norm_g3[1:], final_g.reshape(1, 1, d)], axis=0)

    xs = x.reshape(s, d)
    h = _first_norm(xs, norm_g3)
    for l in range(depth):
        p_mm, p_ew = _project(h, w_in, l)
        ya = _stick_breaking(p_mm, p_ew)
        yb = _conformer_conv(p_ew, conv_w, conv_b3, conv_ln_g3, conv_ln_b3, l)
        yc = _spatial_gating(p_ew, sgu_w, bs_full, sgu_ln_g3, sgu_ln_b3, l)
        yd = _dilated_mixture(p_ew)
        norm_dtype = F32 if l == depth - 1 else BF16
        merged = _gate_merge(h, (ya, yb, yc, yd), w_gate, b_gate3, w_branch, l)
        xs, h = _out_proj(merged, xs, next_g3, w_out_b, l, norm_dtype)
    return h.reshape(batch, s, d)
```
